```python
import jax, jax.numpy as jnp
from jax import lax
import numpy as np

D_MODEL = 1024
BATCH = 16
SEQ = 2048
DEPTH = 4

POOL_GROUPS = 4
POOL_WINDOWS = (2, 4, 8, 16)
POOL_GROUP_DIM = D_MODEL // 8
POOL_WIDTH = POOL_GROUPS * POOL_GROUP_DIM
POOL_OUT_GROUP = D_MODEL // POOL_GROUPS
CONV_WIDTH = D_MODEL // 2
CONV_K = 31
SGU_HEADS = 4
SGU_WIDTH = D_MODEL // 2
SGU_HEAD_DIM = SGU_WIDTH // SGU_HEADS
SGU_CHUNK = 128
N_BRANCH = 3
D_FF = 2816
FFN_CONV_K = 3
RMS_EPS = 1e-6
LN_EPS = 1e-5

IN_SPLITS = (
    POOL_WIDTH,
    POOL_WIDTH + CONV_WIDTH,
    POOL_WIDTH + 2 * CONV_WIDTH,
    POOL_WIDTH + 2 * CONV_WIDTH + SGU_WIDTH,
    POOL_WIDTH + 2 * CONV_WIDTH + 2 * SGU_WIDTH,
)
IN_COLS = POOL_WIDTH + 2 * CONV_WIDTH + 2 * SGU_WIDTH + N_BRANCH * D_MODEL

kernel_name = 'hybrid_pool_conv_sgu_gated_trunk'


def rmsnorm(x, g):
    xf = x.astype(jnp.float32)
    y = xf * lax.rsqrt(jnp.mean(xf * xf, axis=-1, keepdims=True) + RMS_EPS)
    return (y * g.astype(jnp.float32)).astype(x.dtype)


def layernorm(x, g, b):
    xf = x.astype(jnp.float32)
    mu = jnp.mean(xf, axis=-1, keepdims=True)
    var = jnp.mean(jnp.square(xf - mu), axis=-1, keepdims=True)
    y = (xf - mu) * lax.rsqrt(var + LN_EPS)
    return (y * g.astype(jnp.float32) + b.astype(jnp.float32)).astype(x.dtype)


def causal_dwconv(x, w, b):
    k, c = w.shape
    y = lax.conv_general_dilated(
        x, w[:, None, :].astype(x.dtype), window_strides=(1,), padding=[(k - 1, 0)],
        dimension_numbers=('NWC', 'WIO', 'NWC'), feature_group_count=c)
    return y + b.astype(x.dtype)


def pool_mixer(xa, w_grp, scale):
    b, s, _ = xa.shape
    xg = xa.reshape(b, s, POOL_GROUPS, POOL_GROUP_DIM).astype(jnp.float32)
    cs = jnp.cumsum(xg, axis=1)
    wmax = max(POOL_WINDOWS)
    cs_pad = jnp.pad(cs, ((0, 0), (wmax, 0), (0, 0), (0, 0)))
    pos = jnp.arange(s)
    win = jnp.array(POOL_WINDOWS, dtype=jnp.int32)
    idx = pos[:, None] - win[None, :] + wmax
    lower = cs_pad[:, idx, jnp.arange(POOL_GROUPS)[None, :], :]
    cnt = jnp.minimum(pos[:, None] + 1, win[None, :]).astype(jnp.float32)
    pooled = (cs - lower) / cnt[None, :, :, None] - xg
    pooled = pooled.astype(xa.dtype)
    y = jnp.einsum('bsgc,gcd->bsgd', pooled, w_grp)
    return y.reshape(b, s, D_MODEL) * scale


def conv_module(val, gate, w_dw, b_dw, ln_g, ln_b, w_pw):
    h = val * jax.nn.sigmoid(gate)
    h = causal_dwconv(h, w_dw, b_dw)
    h = jax.nn.silu(layernorm(h, ln_g, ln_b))
    return h @ w_pw


def sgu_mixer(u, v, w_s, b_s, ln_g, ln_b, w_o):
    b, s, _ = u.shape
    n_chunk = s // SGU_CHUNK
    u = jax.nn.gelu(u)
    v = layernorm(jax.nn.gelu(v), ln_g, ln_b)
    vc = v.reshape(b, n_chunk, SGU_CHUNK, SGU_HEADS, SGU_HEAD_DIM)
    mask = jnp.tril(jnp.ones((SGU_CHUNK, SGU_CHUNK), dtype=bool))
    ws = jnp.where(mask[None], w_s, 0.0).astype(v.dtype)
    mixed = jnp.einsum('hts,bnshe->bnthe', ws, vc) + b_s.T.astype(v.dtype)[None, None, :, :, None]
    gated = u * mixed.reshape(b, s, SGU_WIDTH)
    return gated @ w_o


def conv_ffn(h, w_up, w_dw, b_dw, w_down):
    z = h @ w_up
    g, val = jnp.split(z, 2, axis=-1)
    g = causal_dwconv(g, w_dw, b_dw)
    return (jax.nn.silu(g) * val) @ w_down


def setup_inputs(seed: int = 0) -> dict:
    key = jax.random.key(seed)
    ks = jax.random.split(key, 24)
    L = DEPTH
    f32 = jnp.float32

    def nrm(k, shape, scale):
        return jax.random.normal(k, shape, dtype=f32) * scale

    return {
        'x': nrm(ks[0], (BATCH, SEQ, D_MODEL), 1.0),
        'norm_mix': 1.0 + nrm(ks[1], (L, D_MODEL), 0.02),
        'w_in': nrm(ks[2], (L, D_MODEL, IN_COLS), D_MODEL ** -0.5),
        'b_gate': nrm(ks[3], (L, N_BRANCH * D_MODEL), 0.02),
        'pool_w': nrm(ks[4], (L, POOL_GROUPS, POOL_GROUP_DIM, POOL_OUT_GROUP), POOL_GROUP_DIM ** -0.5),
        'pool_scale': 1.0 + nrm(ks[5], (L, D_MODEL), 0.02),
        'conv_dw_w': nrm(ks[6], (L, CONV_K, CONV_WIDTH), CONV_K ** -0.5),
        'conv_dw_b': nrm(ks[7], (L, CONV_WIDTH), 0.02),
        'conv_ln_g': 1.0 + nrm(ks[8], (L, CONV_WIDTH), 0.02),
        'conv_ln_b': nrm(ks[9], (L, CONV_WIDTH), 0.02),
        'conv_pw': nrm(ks[10], (L, CONV_WIDTH, D_MODEL), CONV_WIDTH ** -0.5),
        'sgu_w': nrm(ks[11], (L, SGU_HEADS, SGU_CHUNK, SGU_CHUNK), SGU_CHUNK ** -0.5),
        'sgu_b': 1.0 + nrm(ks[12], (L, SGU_HEADS, SGU_CHUNK), 0.02),
        'sgu_ln_g': 1.0 + nrm(ks[13], (L, SGU_WIDTH), 0.02),
        'sgu_ln_b': nrm(ks[14], (L, SGU_WIDTH), 0.02),
        'sgu_out': nrm(ks[15], (L, SGU_WIDTH, D_MODEL), SGU_WIDTH ** -0.5),
        'w_out': nrm(ks[16], (L, D_MODEL, D_MODEL), D_MODEL ** -0.5),
        'norm_ffn': 1.0 + nrm(ks[17], (L, D_MODEL), 0.02),
        'ffn_up': nrm(ks[18], (L, D_MODEL, 2 * D_FF), D_MODEL ** -0.5),
        'ffn_dw_w': nrm(ks[19], (L, FFN_CONV_K, D_FF), FFN_CONV_K ** -0.5),
        'ffn_dw_b': nrm(ks[20], (L, D_FF), 0.02),
        'ffn_down': nrm(ks[21], (L, D_FF, D_MODEL), D_FF ** -0.5),
        'norm_final': 1.0 + nrm(ks[22], (D_MODEL,), 0.02),
    }


def reference(x, norm_mix, w_in, b_gate, pool_w, pool_scale, conv_dw_w, conv_dw_b,
              conv_ln_g, conv_ln_b, conv_pw, sgu_w, sgu_b, sgu_ln_g, sgu_ln_b, sgu_out,
              w_out, norm_ffn, ffn_up, ffn_dw_w, ffn_dw_b, ffn_down, norm_final):
    b, s, _ = x.shape
    for l in range(DEPTH):
        h = rmsnorm(x, norm_mix[l])
        z = h @ w_in[l]
        z_pool, z_cv, z_cg, z_u, z_v, z_gate = jnp.split(z, IN_SPLITS, axis=-1)
        y_a = pool_mixer(z_pool, pool_w[l], pool_scale[l])
        y_b = conv_module(z_cv, z_cg, conv_dw_w[l], conv_dw_b[l], conv_ln_g[l], conv_ln_b[l], conv_pw[l])
        y_c = sgu_mixer(z_u, z_v, sgu_w[l], sgu_b[l], sgu_ln_g[l], sgu_ln_b[l], sgu_out[l])
        gates = jax.nn.sigmoid(z_gate + b_gate[l]).reshape(b, s, N_BRANCH, D_MODEL)
        merged = gates[:, :, 0] * y_a + gates[:, :, 1] * y_b + gates[:, :, 2] * y_c
        x = x + merged @ w_out[l]
        h = rmsnorm(x, norm_ffn[l])
        x = x + conv_ffn(h, ffn_up[l], ffn_dw_w[l], ffn_dw_b[l], ffn_down[l])
    return rmsnorm(x, norm_final)
```

```python
import functools
import math

import jax
import jax.numpy as jnp
from jax import lax
from jax.experimental import pallas as pl
from jax.experimental.pallas import tpu as pltpu

D_MODEL = 1024
DEPTH = 4
POOL_GROUPS = 4
POOL_WINDOWS = (2, 4, 8, 16)
POOL_GROUP_DIM = 128
POOL_WIDTH = 512
POOL_OUT_GROUP = 256
CONV_WIDTH = 512
CONV_K = 31
SGU_HEADS = 4
SGU_WIDTH = 512
SGU_HEAD_DIM = 128
SGU_CHUNK = 128
D_FF = 2816
FFN_CONV_K = 3
RMS_EPS = 1e-6
LN_EPS = 1e-5

COL_POOL = 0
COL_CV = POOL_WIDTH
COL_CG = COL_CV + CONV_WIDTH
COL_U = COL_CG + CONV_WIDTH
COL_V = COL_U + SGU_WIDTH
COL_GATE = COL_V + SGU_WIDTH
IN_COLS = COL_GATE + 3 * D_MODEL

SUBLANES = 8
SEQ_TILE = 512
POOL_HALO = 16
CONV_HALO = 32
FFN_HALO = SUBLANES
FFN_CHUNK = 256
FFN_NCHUNK = D_FF // FFN_CHUNK
CONV_ROWS = 32
VMEM_LIMIT_BYTES = 56 * 1024 * 1024

F32 = jnp.float32
BF16 = jnp.bfloat16


def _sigmoid(x):
    return 0.5 * jnp.tanh(0.5 * x) + 0.5


def _gelu_tanh(x):
    c = math.sqrt(2.0 / math.pi)
    return 0.5 * x * (1.0 + jnp.tanh(c * (x + 0.044715 * (x * x * x))))


def _rmsnorm(x, g):
    return x * lax.rsqrt(jnp.mean(x * x, axis=-1, keepdims=True) + RMS_EPS) * g


def _layernorm(x, g, b):
    mu = jnp.mean(x, axis=-1, keepdims=True)
    xc = x - mu
    var = jnp.mean(xc * xc, axis=-1, keepdims=True)
    return xc * lax.rsqrt(var + LN_EPS) * g + b


def _dot(a, b):
    return jnp.dot(a, b, preferred_element_type=F32)


def _mixer_kernel(x_ref, nrm_ref, win_ref, bg_ref, pw_ref, ps_ref, cdw_ref, cdb_ref,
                  clg_ref, clb_ref, cpw_ref, sw_ref, sb_ref, slg_ref, slb_ref, so_ref,
                  wo_ref, o_ref, h_sc, pool_sc, conv_sc, hb_sc, u_sc, v_sc, gated_sc, m_sc):
    t = pl.program_id(1)
    T = SEQ_TILE

    @pl.when(t == 0)
    def _():
        pool_sc[0:POOL_HALO, :] = jnp.zeros((POOL_HALO, POOL_WIDTH), F32)
        conv_sc[0:CONV_HALO, :] = jnp.zeros((CONV_HALO, CONV_WIDTH), F32)

    @pl.when(t > 0)
    def _():
        pool_sc[0:POOL_HALO, :] = pool_sc[T:T + POOL_HALO, :]
        conv_sc[0:CONV_HALO, :] = conv_sc[T:T + CONV_HALO, :]

    x = x_ref[...]
    h_sc[...] = _rmsnorm(x, nrm_ref[...]).astype(BF16)

    def proj(lo, width):
        return _dot(h_sc[...], win_ref[:, lo:lo + width])

    def gate(i, lo, width):
        c = COL_GATE + i * D_MODEL + lo
        return _sigmoid(proj(c, width) + bg_ref[:, i * D_MODEL + lo:i * D_MODEL + lo + width])

    pool_sc[POOL_HALO:POOL_HALO + T, :] = proj(COL_POOL, POOL_WIDTH)
    pos = t * T + lax.broadcasted_iota(jnp.int32, (T, 1), 0)
    for g, w in enumerate(POOL_WINDOWS):
        lanes = slice(g * POOL_GROUP_DIM, (g + 1) * POOL_GROUP_DIM)
        cur = pool_sc[POOL_HALO:POOL_HALO + T, lanes]
        s = cur
        for j in range(1, w):
            s = s + pool_sc[POOL_HALO - j:POOL_HALO - j + T, lanes]
        inv_cnt = 1.0 / jnp.minimum(pos + 1, w).astype(F32)
        pooled = (s * inv_cnt - cur).astype(BF16)
        cols = slice(g * POOL_OUT_GROUP, (g + 1) * POOL_OUT_GROUP)
        ya = _dot(pooled, pw_ref[g]) * ps_ref[:, cols]
        m_sc[:, cols] = gate(0, g * POOL_OUT_GROUP, POOL_OUT_GROUP) * ya

    zv = proj(COL_CV, CONV_WIDTH)
    zg = proj(COL_CG, CONV_WIDTH)
    conv_sc[CONV_HALO:CONV_HALO + T, :] = zv * _sigmoid(zg)
    base = CONV_HALO - (CONV_K - 1)
    for r in range(T // CONV_ROWS):
        r0 = r * CONV_ROWS
        acc = conv_sc[base + r0:base + r0 + CONV_ROWS, :] * cdw_ref[0:1, :]
        for k in range(1, CONV_K):
            acc = acc + conv_sc[base + r0 + k:base + r0 + k + CONV_ROWS, :] * cdw_ref[k:k + 1, :]
        acc = _layernorm(acc + cdb_ref[...], clg_ref[...], clb_ref[...])
        hb_sc[r0:r0 + CONV_ROWS, :] = (acc * _sigmoid(acc)).astype(BF16)
    yb = _dot(hb_sc[...], cpw_ref[...])
    m_sc[...] += gate(1, 0, D_MODEL) * yb

    u_sc[...] = _gelu_tanh(proj(COL_U, SGU_WIDTH))
    v = _gelu_tanh(proj(COL_V, SGU_WIDTH))
    v_sc[...] = _layernorm(v, slg_ref[...], slb_ref[...]).astype(BF16)
    row = lax.broadcasted_iota(jnp.int32, (SGU_CHUNK, SGU_CHUNK), 0)
    col = lax.broadcasted_iota(jnp.int32, (SGU_CHUNK, SGU_CHUNK), 1)
    for hd in range(SGU_HEADS):
        lanes = slice(hd * SGU_HEAD_DIM, (hd + 1) * SGU_HEAD_DIM)
        ws = jnp.where(row >= col, sw_ref[hd], 0.0).astype(BF16)
        for n in range(T // SGU_CHUNK):
            rows = slice(n * SGU_CHUNK, (n + 1) * SGU_CHUNK)
            mixed = _dot(ws, v_sc[rows, lanes]) + sb_ref[:, lanes]
            gated_sc[rows, lanes] = (u_sc[rows, lanes] * mixed).astype(BF16)
    yc = _dot(gated_sc[...], so_ref[...])
    m_sc[...] += gate(2, 0, D_MODEL) * yc

    o_ref[...] = x_ref[...] + _dot(m_sc[...].astype(BF16), wo_ref[...])


def _ffn_kernel(x_ref, nrm_ref, wup_ref, dww_ref, dwb_ref, wdn_ref, nf_ref, o_ref,
                h_sc, g_sc, carry_sc, acc_sc, *, final_norm):
    t = pl.program_id(1)
    T = SEQ_TILE

    @pl.when(t == 0)
    def _():
        carry_sc[...] = jnp.zeros(carry_sc.shape, F32)

    x = x_ref[...]
    h_sc[...] = _rmsnorm(x, nrm_ref[...]).astype(BF16)
    acc_sc[...] = x

    def body(c, _):
        z = _dot(h_sc[...], wup_ref[c])
        g_sc[0:FFN_HALO, :] = carry_sc[c]
        g_sc[FFN_HALO:FFN_HALO + T, :] = z[:, :FFN_CHUNK]
        carry_sc[c] = g_sc[T:T + FFN_HALO, :]
        base = FFN_HALO - (FFN_CONV_K - 1)
        w = dww_ref[c]
        g = dwb_ref[c]
        for k in range(FFN_CONV_K):
            g = g + g_sc[base + k:base + k + T, :] * w[k:k + 1, :]
        act = (g * _sigmoid(g) * z[:, FFN_CHUNK:]).astype(BF16)
        acc_sc[...] += _dot(act, wdn_ref[c])
        return 0

    lax.fori_loop(0, FFN_NCHUNK, body, 0)

    y = acc_sc[...]
    if final_norm:
        y = _rmsnorm(y, nf_ref[...])
    o_ref[...] = y


def _const_spec(shape, layer):
    zeros = (0,) * len(shape)
    return pl.BlockSpec((None,) + tuple(shape), lambda b, t: (layer,) + zeros,
                        pipeline_mode=pl.Buffered(1))


def _tile_spec():
    return pl.BlockSpec((None, SEQ_TILE, D_MODEL), lambda b, t: (b, t, 0))


def _compiler_params():
    return pltpu.CompilerParams(dimension_semantics=("arbitrary", "arbitrary"),
                                vmem_limit_bytes=VMEM_LIMIT_BYTES)


def _mixer_call(x, p, layer):
    B, S, _ = x.shape
    T = SEQ_TILE
    consts = [p['norm_mix'], p['w_in'], p['b_gate'], p['pool_w'], p['pool_scale'],
              p['conv_dw_w'], p['conv_dw_b'], p['conv_ln_g'], p['conv_ln_b'], p['conv_pw'],
              p['sgu_w'], p['sgu_b'], p['sgu_ln_g'], p['sgu_ln_b'], p['sgu_out'], p['w_out']]
    return pl.pallas_call(
        _mixer_kernel,
        grid=(B, S // T),
        in_specs=[_tile_spec()] + [_const_spec(a.shape[1:], layer) for a in consts],
        out_specs=_tile_spec(),
        out_shape=jax.ShapeDtypeStruct(x.shape, F32),
        scratch_shapes=[
            pltpu.VMEM((T, D_MODEL), BF16),
            pltpu.VMEM((T + POOL_HALO, POOL_WIDTH), F32),
            pltpu.VMEM((T + CONV_HALO, CONV_WIDTH), F32),
            pltpu.VMEM((T, CONV_WIDTH), BF16),
            pltpu.VMEM((T, SGU_WIDTH), F32),
            pltpu.VMEM((T, SGU_WIDTH), BF16),
            pltpu.VMEM((T, SGU_WIDTH), BF16),
            pltpu.VMEM((T, D_MODEL), F32),
        ],
        compiler_params=_compiler_params(),
        name=f"mixer_l{layer}",
    )(x, *consts)


def _ffn_call(x, p, layer, final_norm):
    B, S, _ = x.shape
    T = SEQ_TILE
    consts = [p['norm_ffn'], p['ffn_up'], p['ffn_dw_w'], p['ffn_dw_b'], p['ffn_down']]
    nf_spec = pl.BlockSpec((1, D_MODEL), lambda b, t: (0, 0))
    return pl.pallas_call(
        functools.partial(_ffn_kernel, final_norm=final_norm),
        grid=(B, S // T),
        in_specs=[_tile_spec()] + [_const_spec(a.shape[1:], layer) for a in consts] + [nf_spec],
        out_specs=_tile_spec(),
        out_shape=jax.ShapeDtypeStruct(x.shape, F32),
        scratch_shapes=[
            pltpu.VMEM((T, D_MODEL), BF16),
            pltpu.VMEM((T + FFN_HALO, FFN_CHUNK), F32),
            pltpu.VMEM((FFN_NCHUNK, FFN_HALO, FFN_CHUNK), F32),
            pltpu.VMEM((T, D_MODEL), F32),
        ],
        compiler_params=_compiler_params(),
        name=f"ffn_l{layer}",
    )(x, *consts, p['norm_final'])


def _prepare_params(norm_mix, w_in, b_gate, pool_w, pool_scale, conv_dw_w, conv_dw_b,
                    conv_ln_g, conv_ln_b, conv_pw, sgu_w, sgu_b, sgu_ln_g, sgu_ln_b, sgu_out,
                    w_out, norm_ffn, ffn_up, ffn_dw_w, ffn_dw_b, ffn_down, norm_final):
    L = DEPTH
    row = lambda a: a.reshape(L, 1, a.shape[-1])
    up = ffn_up.reshape(L, D_MODEL, 2, FFN_NCHUNK, FFN_CHUNK)
    up = up.transpose(0, 3, 1, 2, 4).reshape(L, FFN_NCHUNK, D_MODEL, 2 * FFN_CHUNK)
    return {
        'norm_mix': row(norm_mix),
        'w_in': w_in.astype(BF16),
        'b_gate': row(b_gate),
        'pool_w': pool_w.astype(BF16),
        'pool_scale': row(pool_scale),
        'conv_dw_w': conv_dw_w,
        'conv_dw_b': row(conv_dw_b),
        'conv_ln_g': row(conv_ln_g),
        'conv_ln_b': row(conv_ln_b),
        'conv_pw': conv_pw.astype(BF16),
        'sgu_w': sgu_w,
        'sgu_b': jnp.repeat(sgu_b.transpose(0, 2, 1), SGU_HEAD_DIM, axis=2),
        'sgu_ln_g': row(sgu_ln_g),
        'sgu_ln_b': row(sgu_ln_b),
        'sgu_out': sgu_out.astype(BF16),
        'w_out': w_out.astype(BF16),
        'norm_ffn': row(norm_ffn),
        'ffn_up': up.astype(BF16),
        'ffn_dw_w': ffn_dw_w.reshape(L, FFN_CONV_K, FFN_NCHUNK, FFN_CHUNK).transpose(0, 2, 1, 3),
        'ffn_dw_b': ffn_dw_b.reshape(L, FFN_NCHUNK, 1, FFN_CHUNK),
        'ffn_down': ffn_down.astype(BF16).reshape(L, FFN_NCHUNK, FFN_CHUNK, D_MODEL),
        'norm_final': norm_final.reshape(1, D_MODEL),
    }


def kernel(x, norm_mix, w_in, b_gate, pool_w, pool_scale, conv_dw_w, conv_dw_b, conv_ln_g, conv_ln_b, conv_pw, sgu_w, sgu_b, sgu_ln_g, sgu_ln_b, sgu_out, w_out, norm_ffn, ffn_up, ffn_dw_w, ffn_dw_b, ffn_down, norm_final):
    assert x.shape[1] % SEQ_TILE == 0 and x.shape[2] == D_MODEL
    p = _prepare_params(norm_mix, w_in, b_gate, pool_w, pool_scale, conv_dw_w, conv_dw_b,
                        conv_ln_g, conv_ln_b, conv_pw, sgu_w, sgu_b, sgu_ln_g, sgu_ln_b,
                        sgu_out, w_out, norm_ffn, ffn_up, ffn_dw_w, ffn_dw_b, ffn_down,
                        norm_final)
    for layer in range(DEPTH):
        x = _mixer_call(x, p, layer)
        x = _ffn_call(x, p, layer, final_norm=(layer == DEPTH - 1))
    return x
```

```python
import functools
import math

import jax
import jax.numpy as jnp
from jax import lax
from jax.experimental import pallas as pl
from jax.experimental.pallas import tpu as pltpu

D_MODEL = 1024
DEPTH = 4
POOL_GROUPS = 4
POOL_WINDOWS = (2, 4, 8, 16)
POOL_GROUP_DIM = 128
POOL_WIDTH = 512
POOL_OUT_GROUP = 256
CONV_WIDTH = 512
CONV_K = 31
SGU_HEADS = 4
SGU_WIDTH = 512
SGU_HEAD_DIM = 128
SGU_CHUNK = 128
N_BRANCH = 3
D_FF = 2816
FFN_CONV_K = 3
RMS_EPS = 1e-6
LN_EPS = 1e-5

COL_POOL = 0
COL_CV = POOL_WIDTH
COL_CG = COL_CV + CONV_WIDTH
COL_U = COL_CG + CONV_WIDTH
COL_V = COL_U + SGU_WIDTH
COL_GATE = COL_V + SGU_WIDTH

SUBLANES = 8
LANES = 128
SEQ_TILE = 512
POOL_HALO = 16
CONV_HALO = 32
CONV_GROUPS = CONV_WIDTH // LANES
CONV_ROWS = 64
GATE_BLOCK = N_BRANCH * D_MODEL // CONV_GROUPS
FFN_HALO = SUBLANES
FFN_CHUNK = 256
FFN_NCHUNK = D_FF // FFN_CHUNK
VMEM_LIMIT_BYTES = 56 * 1024 * 1024

F32 = jnp.float32
BF16 = jnp.bfloat16


def _sigmoid(x):
    return 0.5 * jnp.tanh(0.5 * x) + 0.5


def _gelu_tanh(x):
    c = math.sqrt(2.0 / math.pi)
    return 0.5 * x * (1.0 + jnp.tanh(c * (x + 0.044715 * (x * x * x))))


def _rmsnorm(x, g):
    return x * lax.rsqrt(jnp.mean(x * x, axis=-1, keepdims=True) + RMS_EPS) * g


def _layernorm(x, g, b):
    mu = jnp.mean(x, axis=-1, keepdims=True)
    xc = x - mu
    var = jnp.mean(xc * xc, axis=-1, keepdims=True)
    return xc * lax.rsqrt(var + LN_EPS) * g + b


def _dot(a, b):
    return jnp.dot(a, b, preferred_element_type=F32)


def _gate_pieces(lo, hi):
    out = []
    while lo < hi:
        blk, off = divmod(lo, GATE_BLOCK)
        width = min(hi - lo, GATE_BLOCK - off)
        out.append((blk, off, width))
        lo += width
    return out


def _mixer_kernel(x_ref, nrm_ref, win_ref, wg_ref, bg_ref, pw_ref, ps_ref, cdw_ref, cdb_ref,
                  clg_ref, clb_ref, cpw_ref, sw_ref, sb_ref, slg_ref, slb_ref, so_ref,
                  wo_ref, o_ref, h_sc, pool_sc, conv_sc, shift_sc, c_sc, hb_sc, u_sc, v_sc,
                  gated_sc, gate_sc, m_sc):
    t = pl.program_id(1)
    T = SEQ_TILE

    @pl.when(t == 0)
    def _():
        pool_sc[0:POOL_HALO, :] = jnp.zeros((POOL_HALO, POOL_WIDTH), F32)
        conv_sc[:, 0:CONV_HALO, :] = jnp.zeros((CONV_GROUPS, CONV_HALO, LANES), F32)

    @pl.when(t > 0)
    def _():
        pool_sc[0:POOL_HALO, :] = pool_sc[T:T + POOL_HALO, :]
        conv_sc[:, 0:CONV_HALO, :] = conv_sc[:, T:T + CONV_HALO, :]

    h_sc[...] = _rmsnorm(x_ref[...], nrm_ref[...]).astype(BF16)

    def proj(lo, width):
        return _dot(h_sc[...], win_ref[:, lo:lo + width])

    glu = proj(COL_CV, CONV_WIDTH) * _sigmoid(proj(COL_CG, CONV_WIDTH))
    for q in range(CONV_GROUPS):
        conv_sc[q, CONV_HALO:CONV_HALO + T, :] = glu[:, q * LANES:(q + 1) * LANES]
    v = _gelu_tanh(proj(COL_V, SGU_WIDTH))
    v_sc[...] = _layernorm(v, slg_ref[...], slb_ref[...]).astype(BF16)
    u_sc[...] = _gelu_tanh(proj(COL_U, SGU_WIDTH))
    pool_sc[POOL_HALO:POOL_HALO + T, :] = proj(COL_POOL, POOL_WIDTH)

    def gate_conv_step(q):
        gate_sc[q] = _sigmoid(_dot(h_sc[...], wg_ref[q]) + bg_ref[q])
        ext = conv_sc[q]
        for r in range(1, SUBLANES):
            shift_sc[r - 1] = pltpu.roll(ext, r, axis=0)[SUBLANES:, :]
        for r0 in range(0, T, CONV_ROWS):
            acc = cdb_ref[q]
            for d in range(CONV_K):
                a, r = divmod(d, SUBLANES)
                if r == 0:
                    lo = r0 + CONV_HALO - SUBLANES * a
                    src = conv_sc[q, lo:lo + CONV_ROWS, :]
                else:
                    lo = r0 + CONV_HALO - SUBLANES * (a + 1)
                    src = shift_sc[r - 1, lo:lo + CONV_ROWS, :]
                acc = acc + src * cdw_ref[q, CONV_K - 1 - d:CONV_K - d, :]
            c_sc[q, r0:r0 + CONV_ROWS, :] = acc

    for q in range(CONV_GROUPS):
        gate_conv_step(q)

    def gated(i, lo, hi, y):
        parts = [gate_sc[blk, :, off:off + w] for blk, off, w in
                 _gate_pieces(i * D_MODEL + lo, i * D_MODEL + hi)]
        g = parts[0] if len(parts) == 1 else jnp.concatenate(parts, axis=1)
        return g * y

    pos = t * T + lax.broadcasted_iota(jnp.int32, (T, 1), 0)
    for g, w in enumerate(POOL_WINDOWS):
        lanes = slice(g * POOL_GROUP_DIM, (g + 1) * POOL_GROUP_DIM)
        ext = pool_sc[:, lanes]
        s = ext
        span = 1
        while span < w:
            s = s + pltpu.roll(s, span, axis=0)
            span *= 2
        inv_cnt = 1.0 / jnp.minimum(pos + 1, w).astype(F32)
        pooled = (s[POOL_HALO:, :] * inv_cnt - ext[POOL_HALO:, :]).astype(BF16)
        lo = g * POOL_OUT_GROUP
        ya = _dot(pooled, pw_ref[g]) * ps_ref[:, lo:lo + POOL_OUT_GROUP]
        m_sc[:, lo:lo + POOL_OUT_GROUP] = gated(0, lo, lo + POOL_OUT_GROUP, ya)

    conv = jnp.concatenate([c_sc[q] for q in range(CONV_GROUPS)], axis=1)
    hb = _layernorm(conv, clg_ref[...], clb_ref[...])
    hb_sc[...] = (hb * _sigmoid(hb)).astype(BF16)
    m_sc[...] += gated(1, 0, D_MODEL, _dot(hb_sc[...], cpw_ref[...]))

    row = lax.broadcasted_iota(jnp.int32, (SGU_CHUNK, SGU_CHUNK), 0)
    col = lax.broadcasted_iota(jnp.int32, (SGU_CHUNK, SGU_CHUNK), 1)
    for hd in range(SGU_HEADS):
        lanes = slice(hd * SGU_HEAD_DIM, (hd + 1) * SGU_HEAD_DIM)
        ws = jnp.where(row >= col, sw_ref[hd], 0.0).astype(BF16)
        for n in range(T // SGU_CHUNK):
            rows = slice(n * SGU_CHUNK, (n + 1) * SGU_CHUNK)
            mixed = _dot(ws, v_sc[rows, lanes]) + sb_ref[:, lanes]
            gated_sc[rows, lanes] = (u_sc[rows, lanes] * mixed).astype(BF16)
    m_sc[...] += gated(2, 0, D_MODEL, _dot(gated_sc[...], so_ref[...]))

    o_ref[...] = x_ref[...] + _dot(m_sc[...].astype(BF16), wo_ref[...])


def _ffn_kernel(x_ref, nrm_ref, wup_ref, dww_ref, dwb_ref, wdn_ref, nf_ref, o_ref,
                h_sc, g_sc, carry_sc, acc_sc, z_sc, act_sc, *, final_norm):
    t = pl.program_id(1)
    T = SEQ_TILE

    @pl.when(t == 0)
    def _():
        carry_sc[...] = jnp.zeros(carry_sc.shape, F32)

    x = x_ref[...]
    h_sc[...] = _rmsnorm(x, nrm_ref[...]).astype(BF16)
    acc_sc[...] = x

    def up(c):
        z_sc[c % 2] = _dot(h_sc[...], wup_ref[c])

    def gate_act(c):
        zr = z_sc.at[c % 2]
        g_sc[0:FFN_HALO, :] = carry_sc[c]
        g_sc[FFN_HALO:FFN_HALO + T, :] = zr[:, :FFN_CHUNK]
        carry_sc[c] = g_sc[T:T + FFN_HALO, :]
        g = dwb_ref[c] + g_sc[FFN_HALO:FFN_HALO + T, :] * dww_ref[c, FFN_CONV_K - 1:FFN_CONV_K, :]
        for k in range(FFN_CONV_K - 1):
            d = FFN_CONV_K - 1 - k
            g = g + g_sc[FFN_HALO - d:FFN_HALO - d + T, :] * dww_ref[c, k:k + 1, :]
        a = 0.5 * g
        act = (a * zr[:, FFN_CHUNK:]) * (jnp.tanh(a) + 1.0)
        act_sc[:, c * FFN_CHUNK:(c + 1) * FFN_CHUNK] = act.astype(BF16)

    up(0)
    for c in range(FFN_NCHUNK):
        if c + 1 < FFN_NCHUNK:
            up(c + 1)
        gate_act(c)
    acc_sc[...] += _dot(act_sc[...], wdn_ref[...])

    y = acc_sc[...]
    if final_norm:
        y = _rmsnorm(y, nf_ref[...])
    o_ref[...] = y


def _const_spec(shape, layer):
    zeros = (0,) * len(shape)
    return pl.BlockSpec((None,) + tuple(shape), lambda b, t: (layer,) + zeros,
                        pipeline_mode=pl.Buffered(1))


def _tile_spec():
    return pl.BlockSpec((None, SEQ_TILE, D_MODEL), lambda b, t: (b, t, 0))


def _compiler_params():
    return pltpu.CompilerParams(dimension_semantics=("arbitrary", "arbitrary"),
                                vmem_limit_bytes=VMEM_LIMIT_BYTES)


def _mixer_call(x, p, layer):
    B, S, _ = x.shape
    T = SEQ_TILE
    consts = [p['norm_mix'], p['w_in'], p['w_gate'], p['b_gate'], p['pool_w'], p['pool_scale'],
              p['conv_dw_w'], p['conv_dw_b'], p['conv_ln_g'], p['conv_ln_b'], p['conv_pw'],
              p['sgu_w'], p['sgu_b'], p['sgu_ln_g'], p['sgu_ln_b'], p['sgu_out'], p['w_out']]
    return pl.pallas_call(
        _mixer_kernel,
        grid=(B, S // T),
        in_specs=[_tile_spec()] + [_const_spec(a.shape[1:], layer) for a in consts],
        out_specs=_tile_spec(),
        out_shape=jax.ShapeDtypeStruct(x.shape, F32),
        scratch_shapes=[
            pltpu.VMEM((T, D_MODEL), BF16),
            pltpu.VMEM((T + POOL_HALO, POOL_WIDTH), F32),
            pltpu.VMEM((CONV_GROUPS, T + CONV_HALO, LANES), F32),
            pltpu.VMEM((SUBLANES - 1, T + CONV_HALO - SUBLANES, LANES), F32),
            pltpu.VMEM((CONV_GROUPS, T, LANES), F32),
            pltpu.VMEM((T, CONV_WIDTH), BF16),
            pltpu.VMEM((T, SGU_WIDTH), F32),
            pltpu.VMEM((T, SGU_WIDTH), BF16),
            pltpu.VMEM((T, SGU_WIDTH), BF16),
            pltpu.VMEM((CONV_GROUPS, T, GATE_BLOCK), F32),
            pltpu.VMEM((T, D_MODEL), F32),
        ],
        compiler_params=_compiler_params(),
        name=f"mixer_l{layer}",
    )(x, *consts)


def _ffn_call(x, p, layer, final_norm):
    B, S, _ = x.shape
    T = SEQ_TILE
    consts = [p['norm_ffn'], p['ffn_up'], p['ffn_dw_w'], p['ffn_dw_b'], p['ffn_down']]
    nf_spec = pl.BlockSpec((1, D_MODEL), lambda b, t: (0, 0))
    return pl.pallas_call(
        functools.partial(_ffn_kernel, final_norm=final_norm),
        grid=(B, S // T),
        in_specs=[_tile_spec()] + [_const_spec(a.shape[1:], layer) for a in consts] + [nf_spec],
        out_specs=_tile_spec(),
        out_shape=jax.ShapeDtypeStruct(x.shape, F32),
        scratch_shapes=[
            pltpu.VMEM((T, D_MODEL), BF16),
            pltpu.VMEM((T + FFN_HALO, FFN_CHUNK), F32),
            pltpu.VMEM((FFN_NCHUNK, FFN_HALO, FFN_CHUNK), F32),
            pltpu.VMEM((T, D_MODEL), F32),
            pltpu.VMEM((2, T, 2 * FFN_CHUNK), F32),
            pltpu.VMEM((T, D_FF), BF16),
        ],
        compiler_params=_compiler_params(),
        name=f"ffn_l{layer}",
    )(x, *consts, p['norm_final'])


def _prepare_params(norm_mix, w_in, b_gate, pool_w, pool_scale, conv_dw_w, conv_dw_b,
                    conv_ln_g, conv_ln_b, conv_pw, sgu_w, sgu_b, sgu_ln_g, sgu_ln_b, sgu_out,
                    w_out, norm_ffn, ffn_up, ffn_dw_w, ffn_dw_b, ffn_down, norm_final):
    L = DEPTH
    row = lambda a: a.reshape(L, 1, a.shape[-1])
    up = ffn_up.reshape(L, D_MODEL, 2, FFN_NCHUNK, FFN_CHUNK)
    up = up.transpose(0, 3, 1, 2, 4).reshape(L, FFN_NCHUNK, D_MODEL, 2 * FFN_CHUNK)
    w_gate = w_in[:, :, COL_GATE:].reshape(L, D_MODEL, CONV_GROUPS, GATE_BLOCK)
    return {
        'norm_mix': row(norm_mix),
        'w_in': w_in[:, :, :COL_GATE].astype(BF16),
        'w_gate': w_gate.transpose(0, 2, 1, 3).astype(BF16),
        'b_gate': b_gate.reshape(L, CONV_GROUPS, 1, GATE_BLOCK),
        'pool_w': pool_w.astype(BF16),
        'pool_scale': row(pool_scale),
        'conv_dw_w': conv_dw_w.reshape(L, CONV_K, CONV_GROUPS, LANES).transpose(0, 2, 1, 3),
        'conv_dw_b': conv_dw_b.reshape(L, CONV_GROUPS, 1, LANES),
        'conv_ln_g': row(conv_ln_g),
        'conv_ln_b': row(conv_ln_b),
        'conv_pw': conv_pw.astype(BF16),
        'sgu_w': sgu_w,
        'sgu_b': jnp.repeat(sgu_b.transpose(0, 2, 1), SGU_HEAD_DIM, axis=2),
        'sgu_ln_g': row(sgu_ln_g),
        'sgu_ln_b': row(sgu_ln_b),
        'sgu_out': sgu_out.astype(BF16),
        'w_out': w_out.astype(BF16),
        'norm_ffn': row(norm_ffn),
        'ffn_up': up.astype(BF16),
        'ffn_dw_w': ffn_dw_w.reshape(L, FFN_CONV_K, FFN_NCHUNK, FFN_CHUNK).transpose(0, 2, 1, 3),
        'ffn_dw_b': ffn_dw_b.reshape(L, FFN_NCHUNK, 1, FFN_CHUNK),
        'ffn_down': ffn_down.astype(BF16),
        'norm_final': norm_final.reshape(1, D_MODEL),
    }


def kernel(x, norm_mix, w_in, b_gate, pool_w, pool_scale, conv_dw_w, conv_dw_b, conv_ln_g, conv_ln_b, conv_pw, sgu_w, sgu_b, sgu_ln_g, sgu_ln_b, sgu_out, w_out, norm_ffn, ffn_up, ffn_dw_w, ffn_dw_b, ffn_down, norm_final):
    assert x.shape[1] % SEQ_TILE == 0 and x.shape[2] == D_MODEL
    p = _prepare_params(norm_mix, w_in, b_gate, pool_w, pool_scale, conv_dw_w, conv_dw_b,
                        conv_ln_g, conv_ln_b, conv_pw, sgu_w, sgu_b, sgu_ln_g, sgu_ln_b,
                        sgu_out, w_out, norm_ffn, ffn_up, ffn_dw_w, ffn_dw_b, ffn_down,
                        norm_final)
    for layer in range(DEPTH):
        x = _mixer_call(x, p, layer)
        x = _ffn_call(x, p, layer, final_norm=(layer == DEPTH - 1))
    return x
```

```python
import functools
import math

import jax
import jax.numpy as jnp
from jax import lax
from jax.experimental import pallas as pl
from jax.experimental.pallas import tpu as pltpu

D_MODEL = 1024
DEPTH = 4
POOL_GROUPS = 4
POOL_WINDOWS = (2, 4, 8, 16)
POOL_GROUP_DIM = 128
POOL_WIDTH = 512
POOL_OUT_GROUP = 256
CONV_WIDTH = 512
CONV_K = 31
SGU_HEADS = 4
SGU_WIDTH = 512
SGU_HEAD_DIM = 128
SGU_CHUNK = 128
N_BRANCH = 3
D_FF = 2816
FFN_CONV_K = 3
RMS_EPS = 1e-6
LN_EPS = 1e-5

COL_POOL = 0
COL_CV = POOL_WIDTH
COL_CG = COL_CV + CONV_WIDTH
COL_U = COL_CG + CONV_WIDTH
COL_V = COL_U + SGU_WIDTH
COL_GATE = COL_V + SGU_WIDTH

SUBLANES = 8
LANES = 128
SEQ_TILE = 512
POOL_HALO = 16
CONV_HALO = 32
CONV_GROUPS = CONV_WIDTH // LANES
CONV_ROWS = 64
GATE_BLOCK = N_BRANCH * D_MODEL // CONV_GROUPS
FFN_HALO = SUBLANES
FFN_CHUNK = 256
FFN_NCHUNK = D_FF // FFN_CHUNK
VMEM_LIMIT_BYTES = 56 * 1024 * 1024

F32 = jnp.float32
BF16 = jnp.bfloat16


def _sigmoid(x):
    return 0.5 * jnp.tanh(0.5 * x) + 0.5


def _gelu_tanh(x):
    c = math.sqrt(2.0 / math.pi)
    return 0.5 * x * (1.0 + jnp.tanh(c * (x + 0.044715 * (x * x * x))))


def _rmsnorm(x, g):
    return x * lax.rsqrt(jnp.mean(x * x, axis=-1, keepdims=True) + RMS_EPS) * g


def _layernorm(x, g, b):
    mu = jnp.mean(x, axis=-1, keepdims=True)
    xc = x - mu
    var = jnp.mean(xc * xc, axis=-1, keepdims=True)
    return xc * lax.rsqrt(var + LN_EPS) * g + b


def _dot(a, b):
    return jnp.dot(a, b, preferred_element_type=F32)


def _gate_pieces(lo, hi):
    out = []
    while lo < hi:
        blk, off = divmod(lo, GATE_BLOCK)
        width = min(hi - lo, GATE_BLOCK - off)
        out.append((blk, off, width))
        lo += width
    return out


def _mixer_kernel(x_ref, nrm_ref, win_ref, bg_ref, pw_ref, ps_ref, cdw_ref, cdb_ref,
                  clg_ref, clb_ref, cpw_ref, sw_ref, sb_ref, slg_ref, slb_ref, so_ref,
                  wo_ref, o_ref, h_sc, pool_sc, conv_sc, shift_sc, ctmp_sc, gtmp_sc, c_sc,
                  hb_sc, u_sc, v_sc, gated_sc, gate_sc, m_sc):
    t = pl.program_id(1)
    T = SEQ_TILE

    @pl.when(t == 0)
    def _():
        pool_sc[0:POOL_HALO, :] = jnp.zeros((POOL_HALO, POOL_WIDTH), F32)
        conv_sc[:, 0:CONV_HALO, :] = jnp.zeros((CONV_GROUPS, CONV_HALO, LANES), F32)

    @pl.when(t > 0)
    def _():
        pool_sc[0:POOL_HALO, :] = pool_sc[T:T + POOL_HALO, :]
        conv_sc[:, 0:CONV_HALO, :] = conv_sc[:, T:T + CONV_HALO, :]

    h_sc[...] = _rmsnorm(x_ref[...], nrm_ref[...]).astype(BF16)

    def proj(lo, width):
        return _dot(h_sc[...], win_ref[:, lo:lo + width])

    glu = proj(COL_CV, CONV_WIDTH) * _sigmoid(proj(COL_CG, CONV_WIDTH))
    for q in range(CONV_GROUPS):
        conv_sc[q, CONV_HALO:CONV_HALO + T, :] = glu[:, q * LANES:(q + 1) * LANES]
    v = _gelu_tanh(proj(COL_V, SGU_WIDTH))
    v_sc[...] = _layernorm(v, slg_ref[...], slb_ref[...]).astype(BF16)
    u_sc[...] = _gelu_tanh(proj(COL_U, SGU_WIDTH))
    pool_sc[POOL_HALO:POOL_HALO + T, :] = proj(COL_POOL, POOL_WIDTH)

    def gate_conv_step(q, _):
        ext = conv_sc[q]
        shift_sc[0] = ext[SUBLANES:, :]
        for r in range(1, SUBLANES):
            shift_sc[r] = pltpu.roll(ext, r, axis=0)[SUBLANES:, :]
        for r0 in range(0, T, CONV_ROWS):
            acc = cdb_ref[q]
            for d in range(CONV_K):
                a, r = divmod(d, SUBLANES)
                lo = r0 + CONV_HALO - SUBLANES * (a + 1)
                acc = acc + shift_sc[r, lo:lo + CONV_ROWS, :] * cdw_ref[q, CONV_K - 1 - d:CONV_K - d, :]
            ctmp_sc[r0:r0 + CONV_ROWS, :] = acc
        col = pl.multiple_of(COL_GATE + q * GATE_BLOCK, 2 * LANES)
        gtmp_sc[...] = _sigmoid(_dot(h_sc[...], win_ref[:, pl.ds(col, GATE_BLOCK)]) + bg_ref[q])
        c_sc[q] = ctmp_sc[...]
        gate_sc[q] = gtmp_sc[...]
        return 0

    lax.fori_loop(0, CONV_GROUPS, gate_conv_step, 0)

    def gated(i, lo, hi, y):
        parts = [gate_sc[blk, :, off:off + w] for blk, off, w in
                 _gate_pieces(i * D_MODEL + lo, i * D_MODEL + hi)]
        g = parts[0] if len(parts) == 1 else jnp.concatenate(parts, axis=1)
        return g * y

    pos = t * T + lax.broadcasted_iota(jnp.int32, (T, 1), 0)
    for g, w in enumerate(POOL_WINDOWS):
        lanes = slice(g * POOL_GROUP_DIM, (g + 1) * POOL_GROUP_DIM)
        ext = pool_sc[:, lanes]
        s = ext
        span = 1
        while span < w:
            s = s + pltpu.roll(s, span, axis=0)
            span *= 2
        inv_cnt = 1.0 / jnp.minimum(pos + 1, w).astype(F32)
        pooled = (s[POOL_HALO:, :] * inv_cnt - ext[POOL_HALO:, :]).astype(BF16)
        lo = g * POOL_OUT_GROUP
        ya = _dot(pooled, pw_ref[g]) * ps_ref[:, lo:lo + POOL_OUT_GROUP]
        m_sc[:, lo:lo + POOL_OUT_GROUP] = gated(0, lo, lo + POOL_OUT_GROUP, ya)

    conv = jnp.concatenate([c_sc[q] for q in range(CONV_GROUPS)], axis=1)
    hb = _layernorm(conv, clg_ref[...], clb_ref[...])
    hb_sc[...] = (hb * _sigmoid(hb)).astype(BF16)
    m_sc[...] += gated(1, 0, D_MODEL, _dot(hb_sc[...], cpw_ref[...]))

    row = lax.broadcasted_iota(jnp.int32, (SGU_CHUNK, SGU_CHUNK), 0)
    col = lax.broadcasted_iota(jnp.int32, (SGU_CHUNK, SGU_CHUNK), 1)
    for hd in range(SGU_HEADS):
        lanes = slice(hd * SGU_HEAD_DIM, (hd + 1) * SGU_HEAD_DIM)
        ws = jnp.where(row >= col, sw_ref[hd], 0.0).astype(BF16)
        for n in range(T // SGU_CHUNK):
            rows = slice(n * SGU_CHUNK, (n + 1) * SGU_CHUNK)
            mixed = _dot(ws, v_sc[rows, lanes]) + sb_ref[:, lanes]
            gated_sc[rows, lanes] = (u_sc[rows, lanes] * mixed).astype(BF16)
    m_sc[...] += gated(2, 0, D_MODEL, _dot(gated_sc[...], so_ref[...]))

    o_ref[...] = x_ref[...] + _dot(m_sc[...].astype(BF16), wo_ref[...])


def _ffn_kernel(x_ref, nrm_ref, wup_ref, dww_ref, dwb_ref, wdn_ref, nf_ref, o_ref,
                h_sc, g_sc, carry_sc, acc_sc, z_sc, act_sc, *, final_norm):
    t = pl.program_id(1)
    T = SEQ_TILE

    @pl.when(t == 0)
    def _():
        carry_sc[...] = jnp.zeros(carry_sc.shape, F32)

    x = x_ref[...]
    h_sc[...] = _rmsnorm(x, nrm_ref[...]).astype(BF16)
    acc_sc[...] = x

    def up(c):
        cols = [wup_ref[:, lo + c * FFN_CHUNK:lo + (c + 1) * FFN_CHUNK] for lo in (0, D_FF)]
        z_sc[c % 2] = _dot(h_sc[...], jnp.concatenate(cols, axis=1))

    def gate_act(c):
        zr = z_sc.at[c % 2]
        g_sc[0:FFN_HALO, :] = carry_sc[c]
        g_sc[FFN_HALO:FFN_HALO + T, :] = zr[:, :FFN_CHUNK]
        carry_sc[c] = g_sc[T:T + FFN_HALO, :]
        g = dwb_ref[c] + g_sc[FFN_HALO:FFN_HALO + T, :] * dww_ref[c, FFN_CONV_K - 1:FFN_CONV_K, :]
        for k in range(FFN_CONV_K - 1):
            d = FFN_CONV_K - 1 - k
            g = g + g_sc[FFN_HALO - d:FFN_HALO - d + T, :] * dww_ref[c, k:k + 1, :]
        a = 0.5 * g
        act = (a * zr[:, FFN_CHUNK:]) * (jnp.tanh(a) + 1.0)
        act_sc[:, c * FFN_CHUNK:(c + 1) * FFN_CHUNK] = act.astype(BF16)

    up(0)
    for c in range(FFN_NCHUNK):
        if c + 1 < FFN_NCHUNK:
            up(c + 1)
        gate_act(c)
    acc_sc[...] += _dot(act_sc[...], wdn_ref[...])

    y = acc_sc[...]
    if final_norm:
        y = _rmsnorm(y, nf_ref[...])
    o_ref[...] = y


def _const_spec(shape, layer):
    zeros = (0,) * len(shape)
    return pl.BlockSpec((None,) + tuple(shape), lambda b, t: (layer,) + zeros,
                        pipeline_mode=pl.Buffered(1))


def _tile_spec():
    return pl.BlockSpec((None, SEQ_TILE, D_MODEL), lambda b, t: (b, t, 0))


def _compiler_params():
    return pltpu.CompilerParams(dimension_semantics=("arbitrary", "arbitrary"),
                                vmem_limit_bytes=VMEM_LIMIT_BYTES)


def _mixer_call(x, p, layer):
    B, S, _ = x.shape
    T = SEQ_TILE
    consts = [p['norm_mix'], p['w_in'], p['b_gate'], p['pool_w'], p['pool_scale'],
              p['conv_dw_w'], p['conv_dw_b'], p['conv_ln_g'], p['conv_ln_b'], p['conv_pw'],
              p['sgu_w'], p['sgu_b'], p['sgu_ln_g'], p['sgu_ln_b'], p['sgu_out'], p['w_out']]
    return pl.pallas_call(
        _mixer_kernel,
        grid=(B, S // T),
        in_specs=[_tile_spec()] + [_const_spec(a.shape[1:], layer) for a in consts],
        out_specs=_tile_spec(),
        out_shape=jax.ShapeDtypeStruct(x.shape, F32),
        scratch_shapes=[
            pltpu.VMEM((T, D_MODEL), BF16),
            pltpu.VMEM((T + POOL_HALO, POOL_WIDTH), F32),
            pltpu.VMEM((CONV_GROUPS, T + CONV_HALO, LANES), F32),
            pltpu.VMEM((SUBLANES, T + CONV_HALO - SUBLANES, LANES), F32),
            pltpu.VMEM((T, LANES), F32),
            pltpu.VMEM((T, GATE_BLOCK), F32),
            pltpu.VMEM((CONV_GROUPS, T, LANES), F32),
            pltpu.VMEM((T, CONV_WIDTH), BF16),
            pltpu.VMEM((T, SGU_WIDTH), F32),
            pltpu.VMEM((T, SGU_WIDTH), BF16),
            pltpu.VMEM((T, SGU_WIDTH), BF16),
            pltpu.VMEM((CONV_GROUPS, T, GATE_BLOCK), F32),
            pltpu.VMEM((T, D_MODEL), F32),
        ],
        compiler_params=_compiler_params(),
        name=f"mixer_l{layer}",
    )(x, *consts)


def _ffn_call(x, p, layer, final_norm):
    B, S, _ = x.shape
    T = SEQ_TILE
    consts = [p['norm_ffn'], p['ffn_up'], p['ffn_dw_w'], p['ffn_dw_b'], p['ffn_down']]
    nf_spec = pl.BlockSpec((1, D_MODEL), lambda b, t: (0, 0))
    return pl.pallas_call(
        functools.partial(_ffn_kernel, final_norm=final_norm),
        grid=(B, S // T),
        in_specs=[_tile_spec()] + [_const_spec(a.shape[1:], layer) for a in consts] + [nf_spec],
        out_specs=_tile_spec(),
        out_shape=jax.ShapeDtypeStruct(x.shape, F32),
        scratch_shapes=[
            pltpu.VMEM((T, D_MODEL), BF16),
            pltpu.VMEM((T + FFN_HALO, FFN_CHUNK), F32),
            pltpu.VMEM((FFN_NCHUNK, FFN_HALO, FFN_CHUNK), F32),
            pltpu.VMEM((T, D_MODEL), F32),
            pltpu.VMEM((2, T, 2 * FFN_CHUNK), F32),
            pltpu.VMEM((T, D_FF), BF16),
        ],
        compiler_params=_compiler_params(),
        name=f"ffn_l{layer}",
    )(x, *consts, p['norm_final'])


def _prepare_params(norm_mix, w_in, b_gate, pool_w, pool_scale, conv_dw_w, conv_dw_b,
                    conv_ln_g, conv_ln_b, conv_pw, sgu_w, sgu_b, sgu_ln_g, sgu_ln_b, sgu_out,
                    w_out, norm_ffn, ffn_up, ffn_dw_w, ffn_dw_b, ffn_down, norm_final):
    L = DEPTH
    row = lambda a: a.reshape(L, 1, a.shape[-1])
    return {
        'norm_mix': row(norm_mix),
        'w_in': w_in.astype(BF16),
        'b_gate': b_gate.reshape(L, CONV_GROUPS, 1, GATE_BLOCK),
        'pool_w': pool_w.astype(BF16),
        'pool_scale': row(pool_scale),
        'conv_dw_w': conv_dw_w.reshape(L, CONV_K, CONV_GROUPS, LANES).transpose(0, 2, 1, 3),
        'conv_dw_b': conv_dw_b.reshape(L, CONV_GROUPS, 1, LANES),
        'conv_ln_g': row(conv_ln_g),
        'conv_ln_b': row(conv_ln_b),
        'conv_pw': conv_pw.astype(BF16),
        'sgu_w': sgu_w,
        'sgu_b': jnp.repeat(sgu_b.transpose(0, 2, 1), SGU_HEAD_DIM, axis=2),
        'sgu_ln_g': row(sgu_ln_g),
        'sgu_ln_b': row(sgu_ln_b),
        'sgu_out': sgu_out.astype(BF16),
        'w_out': w_out.astype(BF16),
        'norm_ffn': row(norm_ffn),
        'ffn_up': ffn_up.astype(BF16),
        'ffn_dw_w': ffn_dw_w.reshape(L, FFN_CONV_K, FFN_NCHUNK, FFN_CHUNK).transpose(0, 2, 1, 3),
        'ffn_dw_b': ffn_dw_b.reshape(L, FFN_NCHUNK, 1, FFN_CHUNK),
        'ffn_down': ffn_down.astype(BF16),
        'norm_final': norm_final.reshape(1, D_MODEL),
    }


def kernel(x, norm_mix, w_in, b_gate, pool_w, pool_scale, conv_dw_w, conv_dw_b, conv_ln_g, conv_ln_b, conv_pw, sgu_w, sgu_b, sgu_ln_g, sgu_ln_b, sgu_out, w_out, norm_ffn, ffn_up, ffn_dw_w, ffn_dw_b, ffn_down, norm_final):
    assert x.shape[1] % SEQ_TILE == 0 and x.shape[2] == D_MODEL
    p = _prepare_params(norm_mix, w_in, b_gate, pool_w, pool_scale, conv_dw_w, conv_dw_b,
                        conv_ln_g, conv_ln_b, conv_pw, sgu_w, sgu_b, sgu_ln_g, sgu_ln_b,
                        sgu_out, w_out, norm_ffn, ffn_up, ffn_dw_w, ffn_dw_b, ffn_down,
                        norm_final)
    for layer in range(DEPTH):
        x = _mixer_call(x, p, layer)
        x = _ffn_call(x, p, layer, final_norm=(layer == DEPTH - 1))
    return x
```

```python
import functools
import math

import jax
import jax.numpy as jnp
from jax import lax
from jax.experimental import pallas as pl
from jax.experimental.pallas import tpu as pltpu

D_MODEL = 1024
DEPTH = 4
POOL_GROUPS = 4
POOL_WINDOWS = (2, 4, 8, 16)
POOL_GROUP_DIM = 128
POOL_WIDTH = 512
POOL_OUT_GROUP = 256
CONV_WIDTH = 512
CONV_K = 31
SGU_HEADS = 4
SGU_WIDTH = 512
SGU_HEAD_DIM = 128
SGU_CHUNK = 128
N_BRANCH = 3
D_FF = 2816
FFN_CONV_K = 3
RMS_EPS = 1e-6
LN_EPS = 1e-5

COL_POOL = 0
COL_CV = POOL_WIDTH
COL_CG = COL_CV + CONV_WIDTH
COL_U = COL_CG + CONV_WIDTH
COL_V = COL_U + SGU_WIDTH
COL_GATE = COL_V + SGU_WIDTH

SUBLANES = 8
LANES = 128
MIX_TILE = 512
FFN_TILE = 1024
POOL_HALO = 16
CONV_HALO = 32
CONV_GROUPS = CONV_WIDTH // LANES
BUILD_ROWS = 64
CONV_ROWS = 64
GATE_BLOCK = N_BRANCH * D_MODEL // CONV_GROUPS
FFN_HALO = SUBLANES
FFN_CHUNK = 256
FFN_NCHUNK = D_FF // FFN_CHUNK
VMEM_LIMIT_BYTES = 56 * 1024 * 1024

F32 = jnp.float32
BF16 = jnp.bfloat16


def _sigmoid(x):
    return 0.5 * jnp.tanh(0.5 * x) + 0.5


def _gelu_tanh(x):
    c = math.sqrt(2.0 / math.pi)
    return 0.5 * x * (1.0 + jnp.tanh(c * (x + 0.044715 * (x * x * x))))


def _rmsnorm(x, g):
    return x * lax.rsqrt(jnp.mean(x * x, axis=-1, keepdims=True) + RMS_EPS) * g


def _layernorm(x, g, b):
    mu = jnp.mean(x, axis=-1, keepdims=True)
    xc = x - mu
    var = jnp.mean(xc * xc, axis=-1, keepdims=True)
    return xc * lax.rsqrt(var + LN_EPS) * g + b


def _dot(a, b):
    return jnp.dot(a, b, preferred_element_type=F32)


def _gate_pieces(lo, hi):
    out = []
    while lo < hi:
        blk, off = divmod(lo, GATE_BLOCK)
        width = min(hi - lo, GATE_BLOCK - off)
        out.append((blk, off, width))
        lo += width
    return out


def _mixer_kernel(x_ref, nrm_ref, win_ref, bg_ref, pw_ref, ps_ref, cdw_ref, cdb_ref,
                  clg_ref, clb_ref, cpw_ref, sw_ref, sb_ref, slg_ref, slb_ref, so_ref,
                  wo_ref, o_ref, h_sc, pool_sc, conv_sc, shift_sc, ctmp_sc, gtmp_sc, c_sc,
                  hb_sc, u_sc, v_sc, gated_sc, gate_sc, pooled_sc):
    t = pl.program_id(1)
    T = MIX_TILE

    @pl.when(t == 0)
    def _():
        pool_sc[0:POOL_HALO, :] = jnp.zeros((POOL_HALO, POOL_WIDTH), F32)
        conv_sc[:, 0:CONV_HALO, :] = jnp.zeros((CONV_GROUPS, CONV_HALO, LANES), F32)

    @pl.when(t > 0)
    def _():
        pool_sc[0:POOL_HALO, :] = pool_sc[T:T + POOL_HALO, :]
        conv_sc[:, 0:CONV_HALO, :] = conv_sc[:, T:T + CONV_HALO, :]

    h_sc[...] = _rmsnorm(x_ref[...], nrm_ref[...]).astype(BF16)

    def proj(lo, width):
        return _dot(h_sc[...], win_ref[:, lo:lo + width])

    glu = proj(COL_CV, CONV_WIDTH) * _sigmoid(proj(COL_CG, CONV_WIDTH))
    for q in range(CONV_GROUPS):
        conv_sc[q, CONV_HALO:CONV_HALO + T, :] = glu[:, q * LANES:(q + 1) * LANES]
    v = _gelu_tanh(proj(COL_V, SGU_WIDTH))
    v_sc[...] = _layernorm(v, slg_ref[...], slb_ref[...]).astype(BF16)
    u_sc[...] = _gelu_tanh(proj(COL_U, SGU_WIDTH))
    pool_sc[POOL_HALO:POOL_HALO + T, :] = proj(COL_POOL, POOL_WIDTH)

    def gate_conv_step(q, _):
        for c0 in range(0, T + CONV_HALO - SUBLANES, BUILD_ROWS):
            n = min(BUILD_ROWS, T + CONV_HALO - SUBLANES - c0)
            ext = conv_sc[q, c0:c0 + n + SUBLANES, :]
            shift_sc[0, c0:c0 + n, :] = ext[SUBLANES:, :]
            for r in range(1, SUBLANES):
                shift_sc[r, c0:c0 + n, :] = pltpu.roll(ext, r, axis=0)[SUBLANES:, :]
        for r0 in range(0, T, CONV_ROWS):
            acc = cdb_ref[q]
            for d in range(CONV_K):
                a, r = divmod(d, SUBLANES)
                lo = r0 + CONV_HALO - SUBLANES * (a + 1)
                acc = acc + shift_sc[r, lo:lo + CONV_ROWS, :] * cdw_ref[q, CONV_K - 1 - d:CONV_K - d, :]
            ctmp_sc[r0:r0 + CONV_ROWS, :] = acc
        col = pl.multiple_of(COL_GATE + q * GATE_BLOCK, 2 * LANES)
        gtmp_sc[...] = _sigmoid(_dot(h_sc[...], win_ref[:, pl.ds(col, GATE_BLOCK)]) + bg_ref[q])
        c_sc[q] = ctmp_sc[...]
        gate_sc[q] = gtmp_sc[...]
        return 0

    lax.fori_loop(0, CONV_GROUPS, gate_conv_step, 0)

    def gated(i, lo, hi, y):
        parts = [gate_sc[blk, :, off:off + w] for blk, off, w in
                 _gate_pieces(i * D_MODEL + lo, i * D_MODEL + hi)]
        g = parts[0] if len(parts) == 1 else jnp.concatenate(parts, axis=1)
        return g * y

    pos = t * T + lax.broadcasted_iota(jnp.int32, (T, 1), 0)
    for g, w in enumerate(POOL_WINDOWS):
        lanes = slice(g * POOL_GROUP_DIM, (g + 1) * POOL_GROUP_DIM)
        ext = pool_sc[:, lanes]
        s = ext
        span = 1
        while span < w:
            s = s + pltpu.roll(s, span, axis=0)
            span *= 2
        inv_cnt = 1.0 / jnp.minimum(pos + 1, w).astype(F32)
        pooled_sc[:, lanes] = (s[POOL_HALO:, :] * inv_cnt - ext[POOL_HALO:, :]).astype(BF16)

    conv = jnp.concatenate([c_sc[q] for q in range(CONV_GROUPS)], axis=1)
    hb = _layernorm(conv, clg_ref[...], clb_ref[...])
    hb_sc[...] = (hb * _sigmoid(hb)).astype(BF16)

    row = lax.broadcasted_iota(jnp.int32, (SGU_CHUNK, SGU_CHUNK), 0)
    col = lax.broadcasted_iota(jnp.int32, (SGU_CHUNK, SGU_CHUNK), 1)
    for hd in range(SGU_HEADS):
        lanes = slice(hd * SGU_HEAD_DIM, (hd + 1) * SGU_HEAD_DIM)
        ws = jnp.where(row >= col, sw_ref[hd], 0.0).astype(BF16)
        for n in range(T // SGU_CHUNK):
            rows = slice(n * SGU_CHUNK, (n + 1) * SGU_CHUNK)
            mixed = _dot(ws, v_sc[rows, lanes]) + sb_ref[:, lanes]
            gated_sc[rows, lanes] = (u_sc[rows, lanes] * mixed).astype(BF16)

    out = x_ref[...]
    for g in range(POOL_GROUPS):
        lo, hi = g * POOL_OUT_GROUP, (g + 1) * POOL_OUT_GROUP
        ya = _dot(pooled_sc[:, g * POOL_GROUP_DIM:(g + 1) * POOL_GROUP_DIM], pw_ref[g]) * ps_ref[:, lo:hi]
        yb = _dot(hb_sc[...], cpw_ref[:, lo:hi])
        yc = _dot(gated_sc[...], so_ref[:, lo:hi])
        merged = gated(0, lo, hi, ya) + gated(1, lo, hi, yb) + gated(2, lo, hi, yc)
        out = out + _dot(merged.astype(BF16), wo_ref[lo:hi, :])
    o_ref[...] = out


def _ffn_kernel(x_ref, nrm_ref, wup_ref, dww_ref, dwb_ref, wdn_ref, nf_ref, o_ref,
                h_sc, g_sc, carry_sc, acc_sc, z_sc, act_sc, *, final_norm):
    t = pl.program_id(1)
    T = FFN_TILE

    @pl.when(t == 0)
    def _():
        carry_sc[...] = jnp.zeros(carry_sc.shape, F32)

    x = x_ref[...]
    h_sc[...] = _rmsnorm(x, nrm_ref[...]).astype(BF16)
    acc_sc[...] = x

    def up(c):
        cols = [wup_ref[:, lo + c * FFN_CHUNK:lo + (c + 1) * FFN_CHUNK] for lo in (0, D_FF)]
        z_sc[c % 2] = _dot(h_sc[...], jnp.concatenate(cols, axis=1))

    def gate_act(c):
        zr = z_sc.at[c % 2]
        g_sc[0:FFN_HALO, :] = carry_sc[c]
        g_sc[FFN_HALO:FFN_HALO + T, :] = zr[:, :FFN_CHUNK]
        carry_sc[c] = g_sc[T:T + FFN_HALO, :]
        g = dwb_ref[c] + g_sc[FFN_HALO:FFN_HALO + T, :] * dww_ref[c, FFN_CONV_K - 1:FFN_CONV_K, :]
        for k in range(FFN_CONV_K - 1):
            d = FFN_CONV_K - 1 - k
            g = g + g_sc[FFN_HALO - d:FFN_HALO - d + T, :] * dww_ref[c, k:k + 1, :]
        a = 0.5 * g
        act = (a * zr[:, FFN_CHUNK:]) * (jnp.tanh(a) + 1.0)
        act_sc[:, c * FFN_CHUNK:(c + 1) * FFN_CHUNK] = act.astype(BF16)

    up(0)
    for c in range(FFN_NCHUNK):
        if c + 1 < FFN_NCHUNK:
            up(c + 1)
        gate_act(c)
    acc_sc[...] += _dot(act_sc[...], wdn_ref[...])

    y = acc_sc[...]
    if final_norm:
        y = _rmsnorm(y, nf_ref[...])
    o_ref[...] = y


def _const_spec(shape, layer):
    zeros = (0,) * len(shape)
    return pl.BlockSpec((None,) + tuple(shape), lambda b, t: (layer,) + zeros,
                        pipeline_mode=pl.Buffered(1))


def _tile_spec(tile):
    return pl.BlockSpec((None, tile, D_MODEL), lambda b, t: (b, t, 0))


def _compiler_params():
    return pltpu.CompilerParams(dimension_semantics=("arbitrary", "arbitrary"),
                                vmem_limit_bytes=VMEM_LIMIT_BYTES)


def _mixer_call(x, p, layer):
    B, S, _ = x.shape
    T = MIX_TILE
    consts = [p['norm_mix'], p['w_in'], p['b_gate'], p['pool_w'], p['pool_scale'],
              p['conv_dw_w'], p['conv_dw_b'], p['conv_ln_g'], p['conv_ln_b'], p['conv_pw'],
              p['sgu_w'], p['sgu_b'], p['sgu_ln_g'], p['sgu_ln_b'], p['sgu_out'], p['w_out']]
    return pl.pallas_call(
        _mixer_kernel,
        grid=(B, S // T),
        in_specs=[_tile_spec(T)] + [_const_spec(a.shape[1:], layer) for a in consts],
        out_specs=_tile_spec(T),
        out_shape=jax.ShapeDtypeStruct(x.shape, F32),
        scratch_shapes=[
            pltpu.VMEM((T, D_MODEL), BF16),
            pltpu.VMEM((T + POOL_HALO, POOL_WIDTH), F32),
            pltpu.VMEM((CONV_GROUPS, T + CONV_HALO, LANES), F32),
            pltpu.VMEM((SUBLANES, T + CONV_HALO - SUBLANES, LANES), F32),
            pltpu.VMEM((T, LANES), F32),
            pltpu.VMEM((T, GATE_BLOCK), F32),
            pltpu.VMEM((CONV_GROUPS, T, LANES), F32),
            pltpu.VMEM((T, CONV_WIDTH), BF16),
            pltpu.VMEM((T, SGU_WIDTH), F32),
            pltpu.VMEM((T, SGU_WIDTH), BF16),
            pltpu.VMEM((T, SGU_WIDTH), BF16),
            pltpu.VMEM((CONV_GROUPS, T, GATE_BLOCK), F32),
            pltpu.VMEM((T, POOL_WIDTH), BF16),
        ],
        compiler_params=_compiler_params(),
        name=f"mixer_l{layer}",
    )(x, *consts)


def _ffn_call(x, p, layer, final_norm):
    B, S, _ = x.shape
    T = FFN_TILE
    consts = [p['norm_ffn'], p['ffn_up'], p['ffn_dw_w'], p['ffn_dw_b'], p['ffn_down']]
    nf_spec = pl.BlockSpec((1, D_MODEL), lambda b, t: (0, 0))
    return pl.pallas_call(
        functools.partial(_ffn_kernel, final_norm=final_norm),
        grid=(B, S // T),
        in_specs=[_tile_spec(T)] + [_const_spec(a.shape[1:], layer) for a in consts] + [nf_spec],
        out_specs=_tile_spec(T),
        out_shape=jax.ShapeDtypeStruct(x.shape, F32),
        scratch_shapes=[
            pltpu.VMEM((T, D_MODEL), BF16),
            pltpu.VMEM((T + FFN_HALO, FFN_CHUNK), F32),
            pltpu.VMEM((FFN_NCHUNK, FFN_HALO, FFN_CHUNK), F32),
            pltpu.VMEM((T, D_MODEL), F32),
            pltpu.VMEM((2, T, 2 * FFN_CHUNK), F32),
            pltpu.VMEM((T, D_FF), BF16),
        ],
        compiler_params=_compiler_params(),
        name=f"ffn_l{layer}",
    )(x, *consts, p['norm_final'])


def _prepare_params(norm_mix, w_in, b_gate, pool_w, pool_scale, conv_dw_w, conv_dw_b,
                    conv_ln_g, conv_ln_b, conv_pw, sgu_w, sgu_b, sgu_ln_g, sgu_ln_b, sgu_out,
                    w_out, norm_ffn, ffn_up, ffn_dw_w, ffn_dw_b, ffn_down, norm_final):
    L = DEPTH
    row = lambda a: a.reshape(L, 1, a.shape[-1])
    return {
        'norm_mix': row(norm_mix),
        'w_in': w_in.astype(BF16),
        'b_gate': b_gate.reshape(L, CONV_GROUPS, 1, GATE_BLOCK),
        'pool_w': pool_w.astype(BF16),
        'pool_scale': row(pool_scale),
        'conv_dw_w': conv_dw_w.reshape(L, CONV_K, CONV_GROUPS, LANES).transpose(0, 2, 1, 3),
        'conv_dw_b': conv_dw_b.reshape(L, CONV_GROUPS, 1, LANES),
        'conv_ln_g': row(conv_ln_g),
        'conv_ln_b': row(conv_ln_b),
        'conv_pw': conv_pw.astype(BF16),
        'sgu_w': sgu_w,
        'sgu_b': jnp.repeat(sgu_b.transpose(0, 2, 1), SGU_HEAD_DIM, axis=2),
        'sgu_ln_g': row(sgu_ln_g),
        'sgu_ln_b': row(sgu_ln_b),
        'sgu_out': sgu_out.astype(BF16),
        'w_out': w_out.astype(BF16),
        'norm_ffn': row(norm_ffn),
        'ffn_up': ffn_up.astype(BF16),
        'ffn_dw_w': ffn_dw_w.reshape(L, FFN_CONV_K, FFN_NCHUNK, FFN_CHUNK).transpose(0, 2, 1, 3),
        'ffn_dw_b': ffn_dw_b.reshape(L, FFN_NCHUNK, 1, FFN_CHUNK),
        'ffn_down': ffn_down.astype(BF16),
        'norm_final': norm_final.reshape(1, D_MODEL),
    }


def kernel(x, norm_mix, w_in, b_gate, pool_w, pool_scale, conv_dw_w, conv_dw_b, conv_ln_g, conv_ln_b, conv_pw, sgu_w, sgu_b, sgu_ln_g, sgu_ln_b, sgu_out, w_out, norm_ffn, ffn_up, ffn_dw_w, ffn_dw_b, ffn_down, norm_final):
    assert x.shape[1] % MIX_TILE == 0 and x.shape[1] % FFN_TILE == 0 and x.shape[2] == D_MODEL
    p = _prepare_params(norm_mix, w_in, b_gate, pool_w, pool_scale, conv_dw_w, conv_dw_b,
                        conv_ln_g, conv_ln_b, conv_pw, sgu_w, sgu_b, sgu_ln_g, sgu_ln_b,
                        sgu_out, w_out, norm_ffn, ffn_up, ffn_dw_w, ffn_dw_b, ffn_down,
                        norm_final)
    for layer in range(DEPTH):
        x = _mixer_call(x, p, layer)
        x = _ffn_call(x, p, layer, final_norm=(layer == DEPTH - 1))
    return x
```

```python
import functools
import math

import jax
import jax.numpy as jnp
from jax import lax
from jax.experimental import pallas as pl
from jax.experimental.pallas import tpu as pltpu

D_MODEL = 1024
DEPTH = 4
POOL_GROUPS = 4
POOL_WINDOWS = (2, 4, 8, 16)
POOL_GROUP_DIM = 128
POOL_WIDTH = 512
POOL_OUT_GROUP = 256
CONV_WIDTH = 512
CONV_K = 31
SGU_HEADS = 4
SGU_WIDTH = 512
SGU_HEAD_DIM = 128
SGU_CHUNK = 128
N_BRANCH = 3
D_FF = 2816
FFN_CONV_K = 3
RMS_EPS = 1e-6
LN_EPS = 1e-5

COL_POOL = 0
COL_CV = POOL_WIDTH
COL_CG = COL_CV + CONV_WIDTH
COL_U = COL_CG + CONV_WIDTH
COL_V = COL_U + SGU_WIDTH
COL_GATE = COL_V + SGU_WIDTH

SUBLANES = 8
LANES = 128
MIX_TILE = 512
FFN_TILE = 1024
POOL_HALO = 16
CONV_HALO = 32
CONV_GROUPS = CONV_WIDTH // LANES
HEAD_ROWS = 256
BUILD_ROWS = 64
CONV_ROWS = 64
GATE_BLOCK = N_BRANCH * D_MODEL // CONV_GROUPS
FFN_HALO = SUBLANES
FFN_CHUNK = 256
FFN_NCHUNK = D_FF // FFN_CHUNK
VMEM_LIMIT_BYTES = 56 * 1024 * 1024

F32 = jnp.float32
BF16 = jnp.bfloat16


def _sigmoid(x):
    return 0.5 * jnp.tanh(0.5 * x) + 0.5


def _gelu_tanh(x):
    c = math.sqrt(2.0 / math.pi)
    return (0.5 * x) * (1.0 + jnp.tanh(x * (c + (c * 0.044715) * (x * x))))


def _rmsnorm(x, g):
    return x * lax.rsqrt(jnp.mean(x * x, axis=-1, keepdims=True) + RMS_EPS) * g


def _layernorm(x, g, b):
    mu = jnp.mean(x, axis=-1, keepdims=True)
    xc = x - mu
    var = jnp.mean(xc * xc, axis=-1, keepdims=True)
    return xc * lax.rsqrt(var + LN_EPS) * g + b


def _dot(a, b):
    return jnp.dot(a, b, preferred_element_type=F32)


def _gate_pieces(lo, hi):
    out = []
    while lo < hi:
        blk, off = divmod(lo, GATE_BLOCK)
        width = min(hi - lo, GATE_BLOCK - off)
        out.append((blk, off, width))
        lo += width
    return out


def _mixer_kernel(x_ref, nrm_ref, win_ref, bg_ref, pw_ref, ps_ref, cdw_ref, cdb_ref,
                  clg_ref, clb_ref, cpw_ref, sw_ref, sb_ref, slg_ref, slb_ref, so_ref,
                  wo_ref, o_ref, h_sc, pool_sc, conv_sc, shift_sc, ctmp_sc, gtmp_sc, c_sc,
                  hb_sc, u_sc, v_sc, gated_sc, gate_sc, pooled_sc):
    t = pl.program_id(1)
    T = MIX_TILE

    @pl.when(t == 0)
    def _():
        pool_sc[0:POOL_HALO, :] = jnp.zeros((POOL_HALO, POOL_WIDTH), F32)
        conv_sc[:, 0:CONV_HALO, :] = jnp.zeros((CONV_GROUPS, CONV_HALO, LANES), F32)

    @pl.when(t > 0)
    def _():
        pool_sc[0:POOL_HALO, :] = pool_sc[T:T + POOL_HALO, :]
        conv_sc[:, 0:CONV_HALO, :] = conv_sc[:, T:T + CONV_HALO, :]

    h_sc[...] = _rmsnorm(x_ref[...], nrm_ref[...]).astype(BF16)

    def proj(lo, width):
        return _dot(h_sc[...], win_ref[:, lo:lo + width])

    glu = proj(COL_CV, CONV_WIDTH) * _sigmoid(proj(COL_CG, CONV_WIDTH))
    for q in range(CONV_GROUPS):
        conv_sc[q, CONV_HALO:CONV_HALO + T, :] = glu[:, q * LANES:(q + 1) * LANES]
    v = _gelu_tanh(proj(COL_V, SGU_WIDTH))
    v_sc[...] = _layernorm(v, slg_ref[...], slb_ref[...]).astype(BF16)
    u_sc[...] = _gelu_tanh(proj(COL_U, SGU_WIDTH))
    pool_sc[POOL_HALO:POOL_HALO + T, :] = proj(COL_POOL, POOL_WIDTH)

    def gate_conv_step(q, _):
        for c0 in range(0, T + CONV_HALO, BUILD_ROWS):
            n = min(BUILD_ROWS, T + CONV_HALO - c0)
            ext = conv_sc[q, c0:c0 + n, :]
            for r in range(SUBLANES):
                shift_sc[r, c0 + r:c0 + r + n, :] = ext
        for r0 in range(0, T, CONV_ROWS):
            acc = cdb_ref[q]
            for d in range(CONV_K):
                a, r = divmod(d, SUBLANES)
                lo = r0 + CONV_HALO - SUBLANES * a
                acc = acc + shift_sc[r, lo:lo + CONV_ROWS, :] * cdw_ref[q, CONV_K - 1 - d:CONV_K - d, :]
            ctmp_sc[r0:r0 + CONV_ROWS, :] = acc
        col = pl.multiple_of(COL_GATE + q * GATE_BLOCK, 2 * LANES)
        gtmp_sc[...] = jnp.tanh(0.5 * (_dot(h_sc[...], win_ref[:, pl.ds(col, GATE_BLOCK)]) + bg_ref[q])) + 1.0
        c_sc[q] = ctmp_sc[...]
        gate_sc[q] = gtmp_sc[...]
        return 0

    lax.fori_loop(0, CONV_GROUPS, gate_conv_step, 0)

    def gated(i, lo, hi, y):
        parts = [gate_sc[blk, :, off:off + w] for blk, off, w in
                 _gate_pieces(i * D_MODEL + lo, i * D_MODEL + hi)]
        g = parts[0] if len(parts) == 1 else jnp.concatenate(parts, axis=1)
        return g * y

    pos = t * T + lax.broadcasted_iota(jnp.int32, (T, 1), 0)
    for g, w in enumerate(POOL_WINDOWS):
        lanes = slice(g * POOL_GROUP_DIM, (g + 1) * POOL_GROUP_DIM)
        ext = pool_sc[:, lanes]
        s = ext
        span = 1
        while span < w:
            s = s + pltpu.roll(s, span, axis=0)
            span *= 2
        inv_cnt = 1.0 / jnp.minimum(pos + 1, w).astype(F32)
        pooled_sc[:, lanes] = (s[POOL_HALO:, :] * inv_cnt - ext[POOL_HALO:, :]).astype(BF16)

    conv = jnp.concatenate([c_sc[q] for q in range(CONV_GROUPS)], axis=1)
    hb = _layernorm(conv, clg_ref[...], clb_ref[...])
    hb_sc[...] = (hb * _sigmoid(hb)).astype(BF16)

    row = lax.broadcasted_iota(jnp.int32, (SGU_CHUNK, SGU_CHUNK), 0)
    col = lax.broadcasted_iota(jnp.int32, (SGU_CHUNK, SGU_CHUNK), 1)
    for hd in range(SGU_HEADS):
        lanes = slice(hd * SGU_HEAD_DIM, (hd + 1) * SGU_HEAD_DIM)
        ws = jnp.where(row >= col, sw_ref[hd], 0.0).astype(BF16)
        for n in range(T // SGU_CHUNK):
            rows = slice(n * SGU_CHUNK, (n + 1) * SGU_CHUNK)
            mixed = _dot(ws, v_sc[rows, lanes]) + sb_ref[:, lanes]
            gated_sc[rows, lanes] = (u_sc[rows, lanes] * mixed).astype(BF16)

    out = x_ref[...]
    for g in range(POOL_GROUPS):
        lo, hi = g * POOL_OUT_GROUP, (g + 1) * POOL_OUT_GROUP
        ya = _dot(pooled_sc[:, g * POOL_GROUP_DIM:(g + 1) * POOL_GROUP_DIM], pw_ref[g]) * ps_ref[:, lo:hi]
        yb = _dot(hb_sc[...], cpw_ref[:, lo:hi])
        yc = _dot(gated_sc[...], so_ref[:, lo:hi])
        merged = gated(0, lo, hi, ya) + gated(1, lo, hi, yb) + gated(2, lo, hi, yc)
        out = out + _dot((0.5 * merged).astype(BF16), wo_ref[lo:hi, :])
    o_ref[...] = out


def _ffn_kernel(x_ref, nrm_ref, wup_ref, dww_ref, dwb_ref, wdn_ref, nf_ref, o_ref,
                h_sc, g_sc, carry_sc, acc_sc, z_sc, act_sc, *, final_norm):
    t = pl.program_id(1)
    T = FFN_TILE

    @pl.when(t == 0)
    def _():
        carry_sc[...] = jnp.zeros(carry_sc.shape, F32)

    for r0 in range(0, T, HEAD_ROWS):
        h_sc[r0:r0 + HEAD_ROWS, :] = _rmsnorm(x_ref[r0:r0 + HEAD_ROWS, :], nrm_ref[...]).astype(BF16)
    acc_sc[...] = x_ref[...]

    def up(c):
        cols = [wup_ref[:, lo + c * FFN_CHUNK:lo + (c + 1) * FFN_CHUNK] for lo in (0, D_FF)]
        w = jnp.concatenate(cols, axis=1)
        if c == 0:
            for r0 in range(0, T, HEAD_ROWS):
                z_sc[0, r0:r0 + HEAD_ROWS, :] = _dot(h_sc[r0:r0 + HEAD_ROWS, :], w)
        else:
            z_sc[c % 2] = _dot(h_sc[...], w)

    def gate_act(c):
        zr = z_sc.at[c % 2]
        g_sc[0:FFN_HALO, :] = carry_sc[c]
        g_sc[FFN_HALO:FFN_HALO + T, :] = zr[:, :FFN_CHUNK]
        carry_sc[c] = g_sc[T:T + FFN_HALO, :]
        g = dwb_ref[c] + g_sc[FFN_HALO:FFN_HALO + T, :] * dww_ref[c, FFN_CONV_K - 1:FFN_CONV_K, :]
        for k in range(FFN_CONV_K - 1):
            d = FFN_CONV_K - 1 - k
            g = g + g_sc[FFN_HALO - d:FFN_HALO - d + T, :] * dww_ref[c, k:k + 1, :]
        a = 0.5 * g
        act = (a * zr[:, FFN_CHUNK:]) * (jnp.tanh(a) + 1.0)
        act_sc[:, c * FFN_CHUNK:(c + 1) * FFN_CHUNK] = act.astype(BF16)

    up(0)
    for c in range(FFN_NCHUNK):
        if c + 1 < FFN_NCHUNK:
            up(c + 1)
        gate_act(c)
    acc_sc[...] += _dot(act_sc[...], wdn_ref[...])

    y = acc_sc[...]
    if final_norm:
        y = _rmsnorm(y, nf_ref[...])
    o_ref[...] = y


def _const_spec(shape, layer):
    zeros = (0,) * len(shape)
    return pl.BlockSpec((None,) + tuple(shape), lambda b, t: (layer,) + zeros,
                        pipeline_mode=pl.Buffered(1))


def _tile_spec(tile):
    return pl.BlockSpec((None, tile, D_MODEL), lambda b, t: (b, t, 0))


def _compiler_params():
    return pltpu.CompilerParams(dimension_semantics=("arbitrary", "arbitrary"),
                                vmem_limit_bytes=VMEM_LIMIT_BYTES)


def _mixer_call(x, p, layer):
    B, S, _ = x.shape
    T = MIX_TILE
    consts = [p['norm_mix'], p['w_in'], p['b_gate'], p['pool_w'], p['pool_scale'],
              p['conv_dw_w'], p['conv_dw_b'], p['conv_ln_g'], p['conv_ln_b'], p['conv_pw'],
              p['sgu_w'], p['sgu_b'], p['sgu_ln_g'], p['sgu_ln_b'], p['sgu_out'], p['w_out']]
    return pl.pallas_call(
        _mixer_kernel,
        grid=(B, S // T),
        in_specs=[_tile_spec(T)] + [_const_spec(a.shape[1:], layer) for a in consts],
        out_specs=_tile_spec(T),
        out_shape=jax.ShapeDtypeStruct(x.shape, F32),
        scratch_shapes=[
            pltpu.VMEM((T, D_MODEL), BF16),
            pltpu.VMEM((T + POOL_HALO, POOL_WIDTH), F32),
            pltpu.VMEM((CONV_GROUPS, T + CONV_HALO, LANES), F32),
            pltpu.VMEM((SUBLANES, T + CONV_HALO + SUBLANES, LANES), F32),
            pltpu.VMEM((T, LANES), F32),
            pltpu.VMEM((T, GATE_BLOCK), F32),
            pltpu.VMEM((CONV_GROUPS, T, LANES), F32),
            pltpu.VMEM((T, CONV_WIDTH), BF16),
            pltpu.VMEM((T, SGU_WIDTH), F32),
            pltpu.VMEM((T, SGU_WIDTH), BF16),
            pltpu.VMEM((T, SGU_WIDTH), BF16),
            pltpu.VMEM((CONV_GROUPS, T, GATE_BLOCK), F32),
            pltpu.VMEM((T, POOL_WIDTH), BF16),
        ],
        compiler_params=_compiler_params(),
        name=f"mixer_l{layer}",
    )(x, *consts)


def _ffn_call(x, p, layer, final_norm):
    B, S, _ = x.shape
    T = FFN_TILE
    consts = [p['norm_ffn'], p['ffn_up'], p['ffn_dw_w'], p['ffn_dw_b'], p['ffn_down']]
    nf_spec = pl.BlockSpec((1, D_MODEL), lambda b, t: (0, 0))
    return pl.pallas_call(
        functools.partial(_ffn_kernel, final_norm=final_norm),
        grid=(B, S // T),
        in_specs=[_tile_spec(T)] + [_const_spec(a.shape[1:], layer) for a in consts] + [nf_spec],
        out_specs=_tile_spec(T),
        out_shape=jax.ShapeDtypeStruct(x.shape, F32),
        scratch_shapes=[
            pltpu.VMEM((T, D_MODEL), BF16),
            pltpu.VMEM((T + FFN_HALO, FFN_CHUNK), F32),
            pltpu.VMEM((FFN_NCHUNK, FFN_HALO, FFN_CHUNK), F32),
            pltpu.VMEM((T, D_MODEL), F32),
            pltpu.VMEM((2, T, 2 * FFN_CHUNK), F32),
            pltpu.VMEM((T, D_FF), BF16),
        ],
        compiler_params=_compiler_params(),
        name=f"ffn_l{layer}",
    )(x, *consts, p['norm_final'])


def _prepare_params(norm_mix, w_in, b_gate, pool_w, pool_scale, conv_dw_w, conv_dw_b,
                    conv_ln_g, conv_ln_b, conv_pw, sgu_w, sgu_b, sgu_ln_g, sgu_ln_b, sgu_out,
                    w_out, norm_ffn, ffn_up, ffn_dw_w, ffn_dw_b, ffn_down, norm_final):
    L = DEPTH
    row = lambda a: a.reshape(L, 1, a.shape[-1])
    return {
        'norm_mix': row(norm_mix),
        'w_in': w_in.astype(BF16),
        'b_gate': b_gate.reshape(L, CONV_GROUPS, 1, GATE_BLOCK),
        'pool_w': pool_w.astype(BF16),
        'pool_scale': row(pool_scale),
        'conv_dw_w': conv_dw_w.reshape(L, CONV_K, CONV_GROUPS, LANES).transpose(0, 2, 1, 3),
        'conv_dw_b': conv_dw_b.reshape(L, CONV_GROUPS, 1, LANES),
        'conv_ln_g': row(conv_ln_g),
        'conv_ln_b': row(conv_ln_b),
        'conv_pw': conv_pw.astype(BF16),
        'sgu_w': sgu_w,
        'sgu_b': jnp.repeat(sgu_b.transpose(0, 2, 1), SGU_HEAD_DIM, axis=2),
        'sgu_ln_g': row(sgu_ln_g),
        'sgu_ln_b': row(sgu_ln_b),
        'sgu_out': sgu_out.astype(BF16),
        'w_out': w_out.astype(BF16),
        'norm_ffn': row(norm_ffn),
        'ffn_up': ffn_up.astype(BF16),
        'ffn_dw_w': ffn_dw_w.reshape(L, FFN_CONV_K, FFN_NCHUNK, FFN_CHUNK).transpose(0, 2, 1, 3),
        'ffn_dw_b': ffn_dw_b.reshape(L, FFN_NCHUNK, 1, FFN_CHUNK),
        'ffn_down': ffn_down.astype(BF16),
        'norm_final': norm_final.reshape(1, D_MODEL),
    }


def kernel(x, norm_mix, w_in, b_gate, pool_w, pool_scale, conv_dw_w, conv_dw_b, conv_ln_g, conv_ln_b, conv_pw, sgu_w, sgu_b, sgu_ln_g, sgu_ln_b, sgu_out, w_out, norm_ffn, ffn_up, ffn_dw_w, ffn_dw_b, ffn_down, norm_final):
    assert x.shape[1] % MIX_TILE == 0 and x.shape[1] % FFN_TILE == 0 and x.shape[2] == D_MODEL
    p = _prepare_params(norm_mix, w_in, b_gate, pool_w, pool_scale, conv_dw_w, conv_dw_b,
                        conv_ln_g, conv_ln_b, conv_pw, sgu_w, sgu_b, sgu_ln_g, sgu_ln_b,
                        sgu_out, w_out, norm_ffn, ffn_up, ffn_dw_w, ffn_dw_b, ffn_down,
                        norm_final)
    for layer in range(DEPTH):
        x = _mixer_call(x, p, layer)
        x = _ffn_call(x, p, layer, final_norm=(layer == DEPTH - 1))
    return x
```

```python
import functools
import math

import jax
import jax.numpy as jnp
from jax import lax
from jax.experimental import pallas as pl
from jax.experimental.pallas import tpu as pltpu

D_MODEL = 1024
DEPTH = 4
POOL_GROUPS = 4
POOL_WINDOWS = (2, 4, 8, 16)
POOL_GROUP_DIM = 128
POOL_WIDTH = 512
POOL_OUT_GROUP = 256
CONV_WIDTH = 512
CONV_K = 31
SGU_HEADS = 4
SGU_WIDTH = 512
SGU_HEAD_DIM = 128
SGU_CHUNK = 128
N_BRANCH = 3
D_FF = 2816
FFN_CONV_K = 3
RMS_EPS = 1e-6
LN_EPS = 1e-5

COL_POOL = 0
COL_CV = POOL_WIDTH
COL_CG = COL_CV + CONV_WIDTH
COL_U = COL_CG + CONV_WIDTH
COL_V = COL_U + SGU_WIDTH
COL_GATE = COL_V + SGU_WIDTH

SUBLANES = 8
LANES = 128
MIX_TILE = 512
FFN_TILE = 1024
POOL_HALO = 16
CONV_HALO = 32
CONV_GROUPS = CONV_WIDTH // LANES
HEAD_ROWS = 256
BUILD_ROWS = 64
CONV_ROWS = 64
GATE_BLOCK = N_BRANCH * D_MODEL // CONV_GROUPS
FFN_HALO = SUBLANES
FFN_CHUNK = 256
FFN_NCHUNK = D_FF // FFN_CHUNK
VMEM_LIMIT_BYTES = 56 * 1024 * 1024

F32 = jnp.float32
BF16 = jnp.bfloat16


def _sigmoid(x):
    return 0.5 * jnp.tanh(0.5 * x) + 0.5


def _gelu_tanh(x):
    c = math.sqrt(2.0 / math.pi)
    return (0.5 * x) * (1.0 + jnp.tanh(x * (c + (c * 0.044715) * (x * x))))


def _rmsnorm(x, g):
    return x * lax.rsqrt(jnp.mean(x * x, axis=-1, keepdims=True) + RMS_EPS) * g


def _layernorm(x, g, b):
    mu = jnp.mean(x, axis=-1, keepdims=True)
    xc = x - mu
    var = jnp.mean(xc * xc, axis=-1, keepdims=True)
    return xc * lax.rsqrt(var + LN_EPS) * g + b


def _dot(a, b):
    return jnp.dot(a, b, preferred_element_type=F32)


def _gate_pieces(lo, hi):
    out = []
    while lo < hi:
        blk, off = divmod(lo, GATE_BLOCK)
        width = min(hi - lo, GATE_BLOCK - off)
        out.append((blk, off, width))
        lo += width
    return out


def _mixer_kernel(x_ref, nrm_ref, win_ref, bg_ref, pw_ref, ps_ref, cdw_ref, cdb_ref,
                  clg_ref, clb_ref, cpw_ref, sw_ref, sb_ref, slg_ref, slb_ref, so_ref,
                  wo_ref, o_ref, h_sc, pool_sc, conv_sc, shift_sc, ctmp_sc, wtmp_sc, c_sc,
                  hb_sc, u_sc, v_sc, gated_sc, gate_sc, pooled_sc):
    t = pl.program_id(1)
    T = MIX_TILE

    @pl.when(t == 0)
    def _():
        pool_sc[0:POOL_HALO, :] = jnp.zeros((POOL_HALO, POOL_WIDTH), F32)
        conv_sc[:, 0:CONV_HALO, :] = jnp.zeros((CONV_GROUPS, CONV_HALO, LANES), F32)

    @pl.when(t > 0)
    def _():
        pool_sc[0:POOL_HALO, :] = pool_sc[T:T + POOL_HALO, :]
        conv_sc[:, 0:CONV_HALO, :] = conv_sc[:, T:T + CONV_HALO, :]

    h_sc[...] = _rmsnorm(x_ref[...], nrm_ref[...]).astype(BF16)

    def proj(lo, width):
        return _dot(h_sc[...], win_ref[:, lo:lo + width])

    glu = proj(COL_CV, CONV_WIDTH) * _sigmoid(proj(COL_CG, CONV_WIDTH))
    for q in range(CONV_GROUPS):
        conv_sc[q, CONV_HALO:CONV_HALO + T, :] = glu[:, q * LANES:(q + 1) * LANES]
    v = _gelu_tanh(proj(COL_V, SGU_WIDTH))
    v_sc[...] = _layernorm(v, slg_ref[...], slb_ref[...]).astype(BF16)
    u_sc[...] = _gelu_tanh(proj(COL_U, SGU_WIDTH))
    pool_sc[POOL_HALO:POOL_HALO + T, :] = proj(COL_POOL, POOL_WIDTH)

    def gate_conv_step(q, _):
        for c0 in range(0, T + CONV_HALO - SUBLANES, BUILD_ROWS):
            n = min(BUILD_ROWS, T + CONV_HALO - SUBLANES - c0)
            ext = conv_sc[q, c0:c0 + n + SUBLANES, :]
            shift_sc[0, c0:c0 + n, :] = ext[SUBLANES:, :]
            for r in range(1, SUBLANES):
                shift_sc[r, c0:c0 + n, :] = pltpu.roll(ext, r, axis=0)[SUBLANES:, :]
        wtmp_sc[0:CONV_K, :] = cdw_ref[q]
        wtmp_sc[CONV_K:CONV_K + 1, :] = cdb_ref[q]
        for r0 in range(0, T, CONV_ROWS):
            acc = wtmp_sc[CONV_K:CONV_K + 1, :]
            for d in range(CONV_K):
                a, r = divmod(d, SUBLANES)
                lo = r0 + CONV_HALO - SUBLANES * (a + 1)
                acc = acc + shift_sc[r, lo:lo + CONV_ROWS, :] * wtmp_sc[CONV_K - 1 - d:CONV_K - d, :]
            ctmp_sc[r0:r0 + CONV_ROWS, :] = acc
        col = pl.multiple_of(COL_GATE + q * GATE_BLOCK, 2 * LANES)
        gate_sc[q] = jnp.tanh(0.5 * (_dot(h_sc[...], win_ref[:, pl.ds(col, GATE_BLOCK)]) + bg_ref[q])) + 1.0
        c_sc[q] = ctmp_sc[...]
        return 0

    lax.fori_loop(0, CONV_GROUPS, gate_conv_step, 0)

    def gated(i, lo, hi, y):
        parts = [gate_sc[blk, :, off:off + w] for blk, off, w in
                 _gate_pieces(i * D_MODEL + lo, i * D_MODEL + hi)]
        g = parts[0] if len(parts) == 1 else jnp.concatenate(parts, axis=1)
        return g * y

    pos = t * T + lax.broadcasted_iota(jnp.int32, (T, 1), 0)
    for g, w in enumerate(POOL_WINDOWS):
        lanes = slice(g * POOL_GROUP_DIM, (g + 1) * POOL_GROUP_DIM)
        ext = pool_sc[:, lanes]
        s = ext
        span = 1
        while span < w:
            s = s + pltpu.roll(s, span, axis=0)
            span *= 2
        inv_cnt = 1.0 / jnp.minimum(pos + 1, w).astype(F32)
        pooled_sc[:, lanes] = (s[POOL_HALO:, :] * inv_cnt - ext[POOL_HALO:, :]).astype(BF16)

    conv = jnp.concatenate([c_sc[q] for q in range(CONV_GROUPS)], axis=1)
    hb = _layernorm(conv, clg_ref[...], clb_ref[...])
    hb_sc[...] = (hb * _sigmoid(hb)).astype(BF16)

    row = lax.broadcasted_iota(jnp.int32, (SGU_CHUNK, SGU_CHUNK), 0)
    col = lax.broadcasted_iota(jnp.int32, (SGU_CHUNK, SGU_CHUNK), 1)
    for hd in range(SGU_HEADS):
        lanes = slice(hd * SGU_HEAD_DIM, (hd + 1) * SGU_HEAD_DIM)
        ws = jnp.where(row >= col, sw_ref[hd], 0.0).astype(BF16)
        for n in range(T // SGU_CHUNK):
            rows = slice(n * SGU_CHUNK, (n + 1) * SGU_CHUNK)
            mixed = _dot(ws, v_sc[rows, lanes]) + sb_ref[:, lanes]
            gated_sc[rows, lanes] = (u_sc[rows, lanes] * mixed).astype(BF16)

    out = x_ref[...]
    for g in range(POOL_GROUPS):
        lo, hi = g * POOL_OUT_GROUP, (g + 1) * POOL_OUT_GROUP
        ya = _dot(pooled_sc[:, g * POOL_GROUP_DIM:(g + 1) * POOL_GROUP_DIM], pw_ref[g]) * ps_ref[:, lo:hi]
        yb = _dot(hb_sc[...], cpw_ref[:, lo:hi])
        yc = _dot(gated_sc[...], so_ref[:, lo:hi])
        merged = gated(0, lo, hi, ya) + gated(1, lo, hi, yb) + gated(2, lo, hi, yc)
        out = out + _dot((0.5 * merged).astype(BF16), wo_ref[lo:hi, :])
    o_ref[...] = out


def _ffn_kernel(x_ref, nrm_ref, wup_ref, dww_ref, dwb_ref, wdn_ref, nf_ref, o_ref,
                h_sc, g_sc, carry_sc, acc_sc, z_sc, act_sc, *, final_norm):
    t = pl.program_id(1)
    T = FFN_TILE

    @pl.when(t == 0)
    def _():
        carry_sc[...] = jnp.zeros(carry_sc.shape, F32)

    for r0 in range(0, T, HEAD_ROWS):
        h_sc[r0:r0 + HEAD_ROWS, :] = _rmsnorm(x_ref[r0:r0 + HEAD_ROWS, :], nrm_ref[...]).astype(BF16)
    acc_sc[...] = x_ref[...]

    def up(c):
        cols = [wup_ref[:, lo + c * FFN_CHUNK:lo + (c + 1) * FFN_CHUNK] for lo in (0, D_FF)]
        w = jnp.concatenate(cols, axis=1)
        if c == 0:
            for r0 in range(0, T, HEAD_ROWS):
                z_sc[0, r0:r0 + HEAD_ROWS, :] = _dot(h_sc[r0:r0 + HEAD_ROWS, :], w)
        else:
            z_sc[c % 2] = _dot(h_sc[...], w)

    def gate_act(c):
        zr = z_sc.at[c % 2]
        g_sc[0:FFN_HALO, :] = carry_sc[c]
        g_sc[FFN_HALO:FFN_HALO + T, :] = zr[:, :FFN_CHUNK]
        carry_sc[c] = g_sc[T:T + FFN_HALO, :]
        g = dwb_ref[c] + g_sc[FFN_HALO:FFN_HALO + T, :] * dww_ref[c, FFN_CONV_K - 1:FFN_CONV_K, :]
        for k in range(FFN_CONV_K - 1):
            d = FFN_CONV_K - 1 - k
            g = g + g_sc[FFN_HALO - d:FFN_HALO - d + T, :] * dww_ref[c, k:k + 1, :]
        a = 0.5 * g
        act = (a * zr[:, FFN_CHUNK:]) * (jnp.tanh(a) + 1.0)
        act_sc[:, c * FFN_CHUNK:(c + 1) * FFN_CHUNK] = act.astype(BF16)

    up(0)
    for c in range(FFN_NCHUNK):
        if c + 1 < FFN_NCHUNK:
            up(c + 1)
        gate_act(c)
    acc_sc[...] += _dot(act_sc[...], wdn_ref[...])

    y = acc_sc[...]
    if final_norm:
        y = _rmsnorm(y, nf_ref[...])
    o_ref[...] = y


def _const_spec(shape, layer):
    zeros = (0,) * len(shape)
    return pl.BlockSpec((None,) + tuple(shape), lambda b, t: (layer,) + zeros,
                        pipeline_mode=pl.Buffered(1))


def _tile_spec(tile):
    return pl.BlockSpec((None, tile, D_MODEL), lambda b, t: (b, t, 0))


def _compiler_params():
    return pltpu.CompilerParams(dimension_semantics=("arbitrary", "arbitrary"),
                                vmem_limit_bytes=VMEM_LIMIT_BYTES)


def _mixer_call(x, p, layer):
    B, S, _ = x.shape
    T = MIX_TILE
    consts = [p['norm_mix'], p['w_in'], p['b_gate'], p['pool_w'], p['pool_scale'],
              p['conv_dw_w'], p['conv_dw_b'], p['conv_ln_g'], p['conv_ln_b'], p['conv_pw'],
              p['sgu_w'], p['sgu_b'], p['sgu_ln_g'], p['sgu_ln_b'], p['sgu_out'], p['w_out']]
    return pl.pallas_call(
        _mixer_kernel,
        grid=(B, S // T),
        in_specs=[_tile_spec(T)] + [_const_spec(a.shape[1:], layer) for a in consts],
        out_specs=_tile_spec(T),
        out_shape=jax.ShapeDtypeStruct(x.shape, F32),
        scratch_shapes=[
            pltpu.VMEM((T, D_MODEL), BF16),
            pltpu.VMEM((T + POOL_HALO, POOL_WIDTH), F32),
            pltpu.VMEM((CONV_GROUPS, T + CONV_HALO, LANES), F32),
            pltpu.VMEM((SUBLANES, T + CONV_HALO - SUBLANES, LANES), F32),
            pltpu.VMEM((T, LANES), F32),
            pltpu.VMEM((CONV_HALO, LANES), F32),
            pltpu.VMEM((CONV_GROUPS, T, LANES), F32),
            pltpu.VMEM((T, CONV_WIDTH), BF16),
            pltpu.VMEM((T, SGU_WIDTH), F32),
            pltpu.VMEM((T, SGU_WIDTH), BF16),
            pltpu.VMEM((T, SGU_WIDTH), BF16),
            pltpu.VMEM((CONV_GROUPS, T, GATE_BLOCK), F32),
            pltpu.VMEM((T, POOL_WIDTH), BF16),
        ],
        compiler_params=_compiler_params(),
        name=f"mixer_l{layer}",
    )(x, *consts)


def _ffn_call(x, p, layer, final_norm):
    B, S, _ = x.shape
    T = FFN_TILE
    consts = [p['norm_ffn'], p['ffn_up'], p['ffn_dw_w'], p['ffn_dw_b'], p['ffn_down']]
    nf_spec = pl.BlockSpec((1, D_MODEL), lambda b, t: (0, 0))
    return pl.pallas_call(
        functools.partial(_ffn_kernel, final_norm=final_norm),
        grid=(B, S // T),
        in_specs=[_tile_spec(T)] + [_const_spec(a.shape[1:], layer) for a in consts] + [nf_spec],
        out_specs=_tile_spec(T),
        out_shape=jax.ShapeDtypeStruct(x.shape, F32),
        scratch_shapes=[
            pltpu.VMEM((T, D_MODEL), BF16),
            pltpu.VMEM((T + FFN_HALO, FFN_CHUNK), F32),
            pltpu.VMEM((FFN_NCHUNK, FFN_HALO, FFN_CHUNK), F32),
            pltpu.VMEM((T, D_MODEL), F32),
            pltpu.VMEM((2, T, 2 * FFN_CHUNK), F32),
            pltpu.VMEM((T, D_FF), BF16),
        ],
        compiler_params=_compiler_params(),
        name=f"ffn_l{layer}",
    )(x, *consts, p['norm_final'])


def _prepare_params(norm_mix, w_in, b_gate, pool_w, pool_scale, conv_dw_w, conv_dw_b,
                    conv_ln_g, conv_ln_b, conv_pw, sgu_w, sgu_b, sgu_ln_g, sgu_ln_b, sgu_out,
                    w_out, norm_ffn, ffn_up, ffn_dw_w, ffn_dw_b, ffn_down, norm_final):
    L = DEPTH
    row = lambda a: a.reshape(L, 1, a.shape[-1])
    return {
        'norm_mix': row(norm_mix),
        'w_in': w_in.astype(BF16),
        'b_gate': b_gate.reshape(L, CONV_GROUPS, 1, GATE_BLOCK),
        'pool_w': pool_w.astype(BF16),
        'pool_scale': row(pool_scale),
        'conv_dw_w': conv_dw_w.reshape(L, CONV_K, CONV_GROUPS, LANES).transpose(0, 2, 1, 3),
        'conv_dw_b': conv_dw_b.reshape(L, CONV_GROUPS, 1, LANES),
        'conv_ln_g': row(conv_ln_g),
        'conv_ln_b': row(conv_ln_b),
        'conv_pw': conv_pw.astype(BF16),
        'sgu_w': sgu_w,
        'sgu_b': jnp.repeat(sgu_b.transpose(0, 2, 1), SGU_HEAD_DIM, axis=2),
        'sgu_ln_g': row(sgu_ln_g),
        'sgu_ln_b': row(sgu_ln_b),
        'sgu_out': sgu_out.astype(BF16),
        'w_out': w_out.astype(BF16),
        'norm_ffn': row(norm_ffn),
        'ffn_up': ffn_up.astype(BF16),
        'ffn_dw_w': ffn_dw_w.reshape(L, FFN_CONV_K, FFN_NCHUNK, FFN_CHUNK).transpose(0, 2, 1, 3),
        'ffn_dw_b': ffn_dw_b.reshape(L, FFN_NCHUNK, 1, FFN_CHUNK),
        'ffn_down': ffn_down.astype(BF16),
        'norm_final': norm_final.reshape(1, D_MODEL),
    }


def kernel(x, norm_mix, w_in, b_gate, pool_w, pool_scale, conv_dw_w, conv_dw_b, conv_ln_g, conv_ln_b, conv_pw, sgu_w, sgu_b, sgu_ln_g, sgu_ln_b, sgu_out, w_out, norm_ffn, ffn_up, ffn_dw_w, ffn_dw_b, ffn_down, norm_final):
    assert x.shape[1] % MIX_TILE == 0 and x.shape[1] % FFN_TILE == 0 and x.shape[2] == D_MODEL
    p = _prepare_params(norm_mix, w_in, b_gate, pool_w, pool_scale, conv_dw_w, conv_dw_b,
                        conv_ln_g, conv_ln_b, conv_pw, sgu_w, sgu_b, sgu_ln_g, sgu_ln_b,
                        sgu_out, w_out, norm_ffn, ffn_up, ffn_dw_w, ffn_dw_b, ffn_down,
                        norm_final)
    for layer in range(DEPTH):
        x = _mixer_call(x, p, layer)
        x = _ffn_call(x, p, layer, final_norm=(layer == DEPTH - 1))
    return x
```

```python
import functools
import math

import jax
import jax.numpy as jnp
from jax import lax
from jax.experimental import pallas as pl
from jax.experimental.pallas import tpu as pltpu

D_MODEL = 1024
DEPTH = 4
POOL_GROUPS = 4
POOL_WINDOWS = (2, 4, 8, 16)
POOL_GROUP_DIM = 128
POOL_WIDTH = 512
POOL_OUT_GROUP = 256
CONV_WIDTH = 512
CONV_K = 31
SGU_HEADS = 4
SGU_WIDTH = 512
SGU_HEAD_DIM = 128
SGU_CHUNK = 128
N_BRANCH = 3
D_FF = 2816
FFN_CONV_K = 3
RMS_EPS = 1e-6
LN_EPS = 1e-5

COL_POOL = 0
COL_CV = POOL_WIDTH
COL_CG = COL_CV + CONV_WIDTH
COL_U = COL_CG + CONV_WIDTH
COL_V = COL_U + SGU_WIDTH
COL_GATE = COL_V + SGU_WIDTH

SUBLANES = 8
LANES = 128
MIX_TILE = 512
FFN_TILE = 1024
POOL_HALO = 16
CONV_HALO = 32
CONV_GROUPS = CONV_WIDTH // LANES
HEAD_ROWS = 256
BUILD_ROWS = 64
CONV_ROWS = 64
GATE_BLOCK = N_BRANCH * D_MODEL // CONV_GROUPS
FFN_HALO = SUBLANES
FFN_CHUNK = 256
FFN_NCHUNK = D_FF // FFN_CHUNK
VMEM_LIMIT_BYTES = 56 * 1024 * 1024

F32 = jnp.float32
BF16 = jnp.bfloat16


def _sigmoid(x):
    return 0.5 * jnp.tanh(0.5 * x) + 0.5


def _gelu_tanh(x):
    c = math.sqrt(2.0 / math.pi)
    return (0.5 * x) * (1.0 + jnp.tanh(x * (c + (c * 0.044715) * (x * x))))


def _rmsnorm(x, g):
    return x * lax.rsqrt(jnp.mean(x * x, axis=-1, keepdims=True) + RMS_EPS) * g


def _layernorm(x, g, b):
    mu = jnp.mean(x, axis=-1, keepdims=True)
    xc = x - mu
    var = jnp.mean(xc * xc, axis=-1, keepdims=True)
    return xc * lax.rsqrt(var + LN_EPS) * g + b


def _dot(a, b):
    return jnp.dot(a, b, preferred_element_type=F32)


def _gate_pieces(lo, hi):
    out = []
    while lo < hi:
        blk, off = divmod(lo, GATE_BLOCK)
        width = min(hi - lo, GATE_BLOCK - off)
        out.append((blk, off, width))
        lo += width
    return out


def _mixer_kernel(x_ref, nrm_ref, win_ref, bg_ref, pw_ref, ps_ref, cdw_ref, cdb_ref,
                  clg_ref, clb_ref, cpw_ref, sw_ref, sb_ref, slg_ref, slb_ref, so_ref,
                  wo_ref, o_ref, h_sc, pool_sc, conv_sc, shift_sc, ctmp_sc, wtmp_sc, c_sc,
                  hb_sc, u_sc, v_sc, gated_sc, gate_sc, pooled_sc):
    t = pl.program_id(1)
    T = MIX_TILE

    @pl.when(t == 0)
    def _():
        pool_sc[0:POOL_HALO, :] = jnp.zeros((POOL_HALO, POOL_WIDTH), F32)
        conv_sc[:, 0:CONV_HALO, :] = jnp.zeros((CONV_GROUPS, CONV_HALO, LANES), F32)

    @pl.when(t > 0)
    def _():
        pool_sc[0:POOL_HALO, :] = pool_sc[T:T + POOL_HALO, :]
        conv_sc[:, 0:CONV_HALO, :] = conv_sc[:, T:T + CONV_HALO, :]

    h_sc[...] = _rmsnorm(x_ref[...], nrm_ref[...]).astype(BF16)

    def proj(lo, width):
        return _dot(h_sc[...], win_ref[:, lo:lo + width])

    for lo in range(0, CONV_WIDTH, 2 * LANES):
        w = jnp.concatenate([win_ref[:, COL_CV + lo:COL_CV + lo + 2 * LANES],
                             win_ref[:, COL_CG + lo:COL_CG + lo + 2 * LANES]], axis=1)
        z = _dot(h_sc[...], w)
        glu = z[:, :2 * LANES] * _sigmoid(z[:, 2 * LANES:])
        for k in range(2):
            q = lo // LANES + k
            conv_sc[q, CONV_HALO:CONV_HALO + T, :] = glu[:, k * LANES:(k + 1) * LANES]
    v = _gelu_tanh(proj(COL_V, SGU_WIDTH))
    v_sc[...] = _layernorm(v, slg_ref[...], slb_ref[...]).astype(BF16)
    u_sc[...] = _gelu_tanh(proj(COL_U, SGU_WIDTH))
    pool_sc[POOL_HALO:POOL_HALO + T, :] = proj(COL_POOL, POOL_WIDTH)

    row = lax.broadcasted_iota(jnp.int32, (SGU_CHUNK, SGU_CHUNK), 0)
    col = lax.broadcasted_iota(jnp.int32, (SGU_CHUNK, SGU_CHUNK), 1)
    for hd in range(SGU_HEADS):
        lanes = slice(hd * SGU_HEAD_DIM, (hd + 1) * SGU_HEAD_DIM)
        ws = jnp.where(row >= col, sw_ref[hd], 0.0).astype(BF16)
        for n in range(T // SGU_CHUNK):
            rows = slice(n * SGU_CHUNK, (n + 1) * SGU_CHUNK)
            mixed = _dot(ws, v_sc[rows, lanes]) + sb_ref[:, lanes]
            gated_sc[rows, lanes] = (u_sc[rows, lanes] * mixed).astype(BF16)

    def gate_conv_step(q, _):
        for c0 in range(0, T + CONV_HALO - SUBLANES, BUILD_ROWS):
            n = min(BUILD_ROWS, T + CONV_HALO - SUBLANES - c0)
            ext = conv_sc[q, c0:c0 + n + SUBLANES, :]
            shift_sc[0, c0:c0 + n, :] = ext[SUBLANES:, :]
            for r in range(1, SUBLANES):
                shift_sc[r, c0:c0 + n, :] = pltpu.roll(ext, r, axis=0)[SUBLANES:, :]
        wtmp_sc[0:CONV_K, :] = cdw_ref[q]
        wtmp_sc[CONV_K:CONV_K + 1, :] = cdb_ref[q]
        for r0 in range(0, T, CONV_ROWS):
            acc = wtmp_sc[CONV_K:CONV_K + 1, :]
            for d in range(CONV_K):
                a, r = divmod(d, SUBLANES)
                lo = r0 + CONV_HALO - SUBLANES * (a + 1)
                acc = acc + shift_sc[r, lo:lo + CONV_ROWS, :] * wtmp_sc[CONV_K - 1 - d:CONV_K - d, :]
            ctmp_sc[r0:r0 + CONV_ROWS, :] = acc
        col = pl.multiple_of(COL_GATE + q * GATE_BLOCK, 2 * LANES)
        gate_sc[q] = jnp.tanh(0.5 * (_dot(h_sc[...], win_ref[:, pl.ds(col, GATE_BLOCK)]) + bg_ref[q])) + 1.0
        c_sc[q] = ctmp_sc[...]
        return 0

    lax.fori_loop(0, CONV_GROUPS, gate_conv_step, 0)

    def gated(i, lo, hi, y):
        parts = [gate_sc[blk, :, off:off + w] for blk, off, w in
                 _gate_pieces(i * D_MODEL + lo, i * D_MODEL + hi)]
        g = parts[0] if len(parts) == 1 else jnp.concatenate(parts, axis=1)
        return g * y

    pos = t * T + lax.broadcasted_iota(jnp.int32, (T, 1), 0)
    for g, w in enumerate(POOL_WINDOWS):
        lanes = slice(g * POOL_GROUP_DIM, (g + 1) * POOL_GROUP_DIM)
        ext = pool_sc[:, lanes]
        s = ext
        span = 1
        while span < w:
            s = s + pltpu.roll(s, span, axis=0)
            span *= 2
        inv_cnt = 1.0 / jnp.minimum(pos + 1, w).astype(F32)
        pooled_sc[:, lanes] = (s[POOL_HALO:, :] * inv_cnt - ext[POOL_HALO:, :]).astype(BF16)

    conv = jnp.concatenate([c_sc[q] for q in range(CONV_GROUPS)], axis=1)
    hb = _layernorm(conv, clg_ref[...], clb_ref[...])
    hb_sc[...] = (hb * _sigmoid(hb)).astype(BF16)

    out = x_ref[...]
    for g in range(POOL_GROUPS):
        lo, hi = g * POOL_OUT_GROUP, (g + 1) * POOL_OUT_GROUP
        ya = _dot(pooled_sc[:, g * POOL_GROUP_DIM:(g + 1) * POOL_GROUP_DIM], pw_ref[g]) * ps_ref[:, lo:hi]
        yb = _dot(hb_sc[...], cpw_ref[:, lo:hi])
        yc = _dot(gated_sc[...], so_ref[:, lo:hi])
        merged = gated(0, lo, hi, ya) + gated(1, lo, hi, yb) + gated(2, lo, hi, yc)
        out = out + _dot((0.5 * merged).astype(BF16), wo_ref[lo:hi, :])
    o_ref[...] = out


def _ffn_kernel(x_ref, nrm_ref, wup_ref, dww_ref, dwb_ref, wdn_ref, nf_ref, o_ref,
                h_sc, g_sc, carry_sc, acc_sc, z_sc, act_sc, *, final_norm):
    t = pl.program_id(1)
    T = FFN_TILE

    @pl.when(t == 0)
    def _():
        carry_sc[...] = jnp.zeros(carry_sc.shape, F32)

    for r0 in range(0, T, HEAD_ROWS):
        h_sc[r0:r0 + HEAD_ROWS, :] = _rmsnorm(x_ref[r0:r0 + HEAD_ROWS, :], nrm_ref[...]).astype(BF16)
    acc_sc[...] = x_ref[...]

    def up(c):
        cols = [wup_ref[:, lo + c * FFN_CHUNK:lo + (c + 1) * FFN_CHUNK] for lo in (0, D_FF)]
        w = jnp.concatenate(cols, axis=1)
        if c == 0:
            for r0 in range(0, T, HEAD_ROWS):
                z_sc[0, r0:r0 + HEAD_ROWS, :] = _dot(h_sc[r0:r0 + HEAD_ROWS, :], w)
        else:
            z_sc[c % 2] = _dot(h_sc[...], w)

    def gate_act(c):
        zr = z_sc.at[c % 2]
        g_sc[0:FFN_HALO, :] = carry_sc[c]
        g_sc[FFN_HALO:FFN_HALO + T, :] = zr[:, :FFN_CHUNK]
        carry_sc[c] = g_sc[T:T + FFN_HALO, :]
        g = dwb_ref[c] + g_sc[FFN_HALO:FFN_HALO + T, :] * dww_ref[c, FFN_CONV_K - 1:FFN_CONV_K, :]
        for k in range(FFN_CONV_K - 1):
            d = FFN_CONV_K - 1 - k
            g = g + g_sc[FFN_HALO - d:FFN_HALO - d + T, :] * dww_ref[c, k:k + 1, :]
        a = 0.5 * g
        act = (a * zr[:, FFN_CHUNK:]) * (jnp.tanh(a) + 1.0)
        act_sc[:, c * FFN_CHUNK:(c + 1) * FFN_CHUNK] = act.astype(BF16)

    up(0)
    for c in range(FFN_NCHUNK):
        if c + 1 < FFN_NCHUNK:
            up(c + 1)
        gate_act(c)
    acc_sc[...] += _dot(act_sc[...], wdn_ref[...])

    y = acc_sc[...]
    if final_norm:
        y = _rmsnorm(y, nf_ref[...])
    o_ref[...] = y


def _const_spec(shape, layer):
    zeros = (0,) * len(shape)
    return pl.BlockSpec((None,) + tuple(shape), lambda b, t: (layer,) + zeros,
                        pipeline_mode=pl.Buffered(1))


def _tile_spec(tile):
    return pl.BlockSpec((None, tile, D_MODEL), lambda b, t: (b, t, 0))


def _compiler_params():
    return pltpu.CompilerParams(dimension_semantics=("arbitrary", "arbitrary"),
                                vmem_limit_bytes=VMEM_LIMIT_BYTES)


def _mixer_call(x, p, layer):
    B, S, _ = x.shape
    T = MIX_TILE
    consts = [p['norm_mix'], p['w_in'], p['b_gate'], p['pool_w'], p['pool_scale'],
              p['conv_dw_w'], p['conv_dw_b'], p['conv_ln_g'], p['conv_ln_b'], p['conv_pw'],
              p['sgu_w'], p['sgu_b'], p['sgu_ln_g'], p['sgu_ln_b'], p['sgu_out'], p['w_out']]
    return pl.pallas_call(
        _mixer_kernel,
        grid=(B, S // T),
        in_specs=[_tile_spec(T)] + [_const_spec(a.shape[1:], layer) for a in consts],
        out_specs=_tile_spec(T),
        out_shape=jax.ShapeDtypeStruct(x.shape, F32),
        scratch_shapes=[
            pltpu.VMEM((T, D_MODEL), BF16),
            pltpu.VMEM((T + POOL_HALO, POOL_WIDTH), F32),
            pltpu.VMEM((CONV_GROUPS, T + CONV_HALO, LANES), F32),
            pltpu.VMEM((SUBLANES, T + CONV_HALO - SUBLANES, LANES), F32),
            pltpu.VMEM((T, LANES), F32),
            pltpu.VMEM((CONV_HALO, LANES), F32),
            pltpu.VMEM((CONV_GROUPS, T, LANES), F32),
            pltpu.VMEM((T, CONV_WIDTH), BF16),
            pltpu.VMEM((T, SGU_WIDTH), F32),
            pltpu.VMEM((T, SGU_WIDTH), BF16),
            pltpu.VMEM((T, SGU_WIDTH), BF16),
            pltpu.VMEM((CONV_GROUPS, T, GATE_BLOCK), F32),
            pltpu.VMEM((T, POOL_WIDTH), BF16),
        ],
        compiler_params=_compiler_params(),
        name=f"mixer_l{layer}",
    )(x, *consts)


def _ffn_call(x, p, layer, final_norm):
    B, S, _ = x.shape
    T = FFN_TILE
    consts = [p['norm_ffn'], p['ffn_up'], p['ffn_dw_w'], p['ffn_dw_b'], p['ffn_down']]
    nf_spec = pl.BlockSpec((1, D_MODEL), lambda b, t: (0, 0))
    return pl.pallas_call(
        functools.partial(_ffn_kernel, final_norm=final_norm),
        grid=(B, S // T),
        in_specs=[_tile_spec(T)] + [_const_spec(a.shape[1:], layer) for a in consts] + [nf_spec],
        out_specs=_tile_spec(T),
        out_shape=jax.ShapeDtypeStruct(x.shape, F32),
        scratch_shapes=[
            pltpu.VMEM((T, D_MODEL), BF16),
            pltpu.VMEM((T + FFN_HALO, FFN_CHUNK), F32),
            pltpu.VMEM((FFN_NCHUNK, FFN_HALO, FFN_CHUNK), F32),
            pltpu.VMEM((T, D_MODEL), F32),
            pltpu.VMEM((2, T, 2 * FFN_CHUNK), F32),
            pltpu.VMEM((T, D_FF), BF16),
        ],
        compiler_params=_compiler_params(),
        name=f"ffn_l{layer}",
    )(x, *consts, p['norm_final'])


def _prepare_params(norm_mix, w_in, b_gate, pool_w, pool_scale, conv_dw_w, conv_dw_b,
                    conv_ln_g, conv_ln_b, conv_pw, sgu_w, sgu_b, sgu_ln_g, sgu_ln_b, sgu_out,
                    w_out, norm_ffn, ffn_up, ffn_dw_w, ffn_dw_b, ffn_down, norm_final):
    L = DEPTH
    row = lambda a: a.reshape(L, 1, a.shape[-1])
    return {
        'norm_mix': row(norm_mix),
        'w_in': w_in.astype(BF16),
        'b_gate': b_gate.reshape(L, CONV_GROUPS, 1, GATE_BLOCK),
        'pool_w': pool_w.astype(BF16),
        'pool_scale': row(pool_scale),
        'conv_dw_w': conv_dw_w.reshape(L, CONV_K, CONV_GROUPS, LANES).transpose(0, 2, 1, 3),
        'conv_dw_b': conv_dw_b.reshape(L, CONV_GROUPS, 1, LANES),
        'conv_ln_g': row(conv_ln_g),
        'conv_ln_b': row(conv_ln_b),
        'conv_pw': conv_pw.astype(BF16),
        'sgu_w': sgu_w,
        'sgu_b': jnp.repeat(sgu_b.transpose(0, 2, 1), SGU_HEAD_DIM, axis=2),
        'sgu_ln_g': row(sgu_ln_g),
        'sgu_ln_b': row(sgu_ln_b),
        'sgu_out': sgu_out.astype(BF16),
        'w_out': w_out.astype(BF16),
        'norm_ffn': row(norm_ffn),
        'ffn_up': ffn_up.astype(BF16),
        'ffn_dw_w': ffn_dw_w.reshape(L, FFN_CONV_K, FFN_NCHUNK, FFN_CHUNK).transpose(0, 2, 1, 3),
        'ffn_dw_b': ffn_dw_b.reshape(L, FFN_NCHUNK, 1, FFN_CHUNK),
        'ffn_down': ffn_down.astype(BF16),
        'norm_final': norm_final.reshape(1, D_MODEL),
    }


def kernel(x, norm_mix, w_in, b_gate, pool_w, pool_scale, conv_dw_w, conv_dw_b, conv_ln_g, conv_ln_b, conv_pw, sgu_w, sgu_b, sgu_ln_g, sgu_ln_b, sgu_out, w_out, norm_ffn, ffn_up, ffn_dw_w, ffn_dw_b, ffn_down, norm_final):
    assert x.shape[1] % MIX_TILE == 0 and x.shape[1] % FFN_TILE == 0 and x.shape[2] == D_MODEL
    p = _prepare_params(norm_mix, w_in, b_gate, pool_w, pool_scale, conv_dw_w, conv_dw_b,
                        conv_ln_g, conv_ln_b, conv_pw, sgu_w, sgu_b, sgu_ln_g, sgu_ln_b,
                        sgu_out, w_out, norm_ffn, ffn_up, ffn_dw_w, ffn_dw_b, ffn_down,
                        norm_final)
    for layer in range(DEPTH):
        x = _mixer_call(x, p, layer)
        x = _ffn_call(x, p, layer, final_norm=(layer == DEPTH - 1))
    return x
```

```python
import functools
import math

import jax
import jax.numpy as jnp
from jax import lax
from jax.experimental import pallas as pl
from jax.experimental.pallas import tpu as pltpu

D_MODEL = 1024
DEPTH = 4
POOL_GROUPS = 4
POOL_WINDOWS = (2, 4, 8, 16)
POOL_GROUP_DIM = 128
POOL_WIDTH = 512
POOL_OUT_GROUP = 256
CONV_WIDTH = 512
CONV_K = 31
SGU_HEADS = 4
SGU_WIDTH = 512
SGU_HEAD_DIM = 128
SGU_CHUNK = 128
N_BRANCH = 3
D_FF = 2816
FFN_CONV_K = 3
RMS_EPS = 1e-6
LN_EPS = 1e-5

COL_POOL = 0
COL_CV = POOL_WIDTH
COL_CG = COL_CV + CONV_WIDTH
COL_U = COL_CG + CONV_WIDTH
COL_V = COL_U + SGU_WIDTH
COL_GATE = COL_V + SGU_WIDTH

SUBLANES = 8
LANES = 128
MIX_TILE = 512
FFN_TILE = 1024
POOL_HALO = 16
CONV_HALO = 32
CONV_GROUPS = CONV_WIDTH // LANES
HEAD_ROWS = 256
BUILD_ROWS = 64
CONV_ROWS = 64
GATE_BLOCK = N_BRANCH * D_MODEL // CONV_GROUPS
FFN_HALO = SUBLANES
FFN_CHUNK = 256
FFN_NCHUNK = D_FF // FFN_CHUNK
VMEM_LIMIT_BYTES = 56 * 1024 * 1024

F32 = jnp.float32
BF16 = jnp.bfloat16


def _sigmoid(x):
    return 0.5 * jnp.tanh(0.5 * x) + 0.5


def _gelu_tanh(x):
    c = math.sqrt(2.0 / math.pi)
    return (0.5 * x) * (1.0 + jnp.tanh(x * (c + (c * 0.044715) * (x * x))))


def _rmsnorm(x, g):
    return x * lax.rsqrt(jnp.mean(x * x, axis=-1, keepdims=True) + RMS_EPS) * g


def _layernorm(x, g, b):
    mu = jnp.mean(x, axis=-1, keepdims=True)
    xc = x - mu
    var = jnp.mean(xc * xc, axis=-1, keepdims=True)
    return xc * lax.rsqrt(var + LN_EPS) * g + b


def _dot(a, b):
    return jnp.dot(a, b, preferred_element_type=F32)


def _gate_pieces(lo, hi):
    out = []
    while lo < hi:
        blk, off = divmod(lo, GATE_BLOCK)
        width = min(hi - lo, GATE_BLOCK - off)
        out.append((blk, off, width))
        lo += width
    return out


def _mixer_kernel(x_ref, nrm_ref, win_ref, bg_ref, pw_ref, ps_ref, cdw_ref, cdb_ref,
                  clg_ref, clb_ref, cpw_ref, sw_ref, sb_ref, slg_ref, slb_ref, so_ref,
                  wo_ref, o_ref, h_sc, hh_sc, pool_sc, conv_sc, shift_sc, ctmp_sc, wtmp_sc, c_sc,
                  hb_sc, u_sc, v_sc, gated_sc, gate_sc, pooled_sc):
    t = pl.program_id(1)
    T = MIX_TILE

    @pl.when(t == 0)
    def _():
        pool_sc[0:POOL_HALO, :] = jnp.zeros((POOL_HALO, POOL_WIDTH), F32)
        conv_sc[:, 0:CONV_HALO, :] = jnp.zeros((CONV_GROUPS, CONV_HALO, LANES), F32)

    @pl.when(t > 0)
    def _():
        pool_sc[0:POOL_HALO, :] = pool_sc[T:T + POOL_HALO, :]
        conv_sc[:, 0:CONV_HALO, :] = conv_sc[:, T:T + CONV_HALO, :]

    h = _rmsnorm(x_ref[...], nrm_ref[...])
    h_sc[...] = h.astype(BF16)
    hh_sc[...] = (0.5 * h).astype(BF16)

    def proj(lo, width):
        return _dot(h_sc[...], win_ref[:, lo:lo + width])

    for lo in range(0, CONV_WIDTH, 2 * LANES):
        w = jnp.concatenate([win_ref[:, COL_CV + lo:COL_CV + lo + 2 * LANES],
                             win_ref[:, COL_CG + lo:COL_CG + lo + 2 * LANES]], axis=1)
        z = _dot(h_sc[...], w)
        glu = z[:, :2 * LANES] * _sigmoid(z[:, 2 * LANES:])
        for k in range(2):
            q = lo // LANES + k
            conv_sc[q, CONV_HALO:CONV_HALO + T, :] = glu[:, k * LANES:(k + 1) * LANES]
    v = _gelu_tanh(proj(COL_V, SGU_WIDTH))
    v_sc[...] = _layernorm(v, slg_ref[...], slb_ref[...]).astype(BF16)
    u_sc[...] = _gelu_tanh(proj(COL_U, SGU_WIDTH))
    pool_sc[POOL_HALO:POOL_HALO + T, :] = proj(COL_POOL, POOL_WIDTH)

    row = lax.broadcasted_iota(jnp.int32, (SGU_CHUNK, SGU_CHUNK), 0)
    col = lax.broadcasted_iota(jnp.int32, (SGU_CHUNK, SGU_CHUNK), 1)
    for hd in range(SGU_HEADS):
        lanes = slice(hd * SGU_HEAD_DIM, (hd + 1) * SGU_HEAD_DIM)
        ws = jnp.where(row >= col, sw_ref[hd], 0.0).astype(BF16)
        for n in range(T // SGU_CHUNK):
            rows = slice(n * SGU_CHUNK, (n + 1) * SGU_CHUNK)
            mixed = _dot(ws, v_sc[rows, lanes]) + sb_ref[:, lanes]
            gated_sc[rows, lanes] = (u_sc[rows, lanes] * mixed).astype(BF16)

    def gate_conv_step(q, _):
        for c0 in range(0, T + CONV_HALO - SUBLANES, BUILD_ROWS):
            n = min(BUILD_ROWS, T + CONV_HALO - SUBLANES - c0)
            ext = conv_sc[q, c0:c0 + n + SUBLANES, :]
            shift_sc[0, c0:c0 + n, :] = ext[SUBLANES:, :]
            for r in range(1, SUBLANES):
                shift_sc[r, c0:c0 + n, :] = pltpu.roll(ext, r, axis=0)[SUBLANES:, :]
        wtmp_sc[0:CONV_K, :] = cdw_ref[q]
        wtmp_sc[CONV_K:CONV_K + 1, :] = cdb_ref[q]
        for r0 in range(0, T, CONV_ROWS):
            acc = wtmp_sc[CONV_K:CONV_K + 1, :]
            for d in range(CONV_K):
                a, r = divmod(d, SUBLANES)
                lo = r0 + CONV_HALO - SUBLANES * (a + 1)
                acc = acc + shift_sc[r, lo:lo + CONV_ROWS, :] * wtmp_sc[CONV_K - 1 - d:CONV_K - d, :]
            ctmp_sc[r0:r0 + CONV_ROWS, :] = acc
        col = pl.multiple_of(COL_GATE + q * GATE_BLOCK, 2 * LANES)
        gate_sc[q] = jnp.tanh(_dot(hh_sc[...], win_ref[:, pl.ds(col, GATE_BLOCK)]) + 0.5 * bg_ref[q]) + 1.0
        c_sc[q] = ctmp_sc[...]
        return 0

    lax.fori_loop(0, CONV_GROUPS, gate_conv_step, 0)

    def gated(i, lo, hi, y):
        parts = [gate_sc[blk, :, off:off + w] for blk, off, w in
                 _gate_pieces(i * D_MODEL + lo, i * D_MODEL + hi)]
        g = parts[0] if len(parts) == 1 else jnp.concatenate(parts, axis=1)
        return g * y

    pos = t * T + lax.broadcasted_iota(jnp.int32, (T, 1), 0)
    for g, w in enumerate(POOL_WINDOWS):
        lanes = slice(g * POOL_GROUP_DIM, (g + 1) * POOL_GROUP_DIM)
        ext = pool_sc[:, lanes]
        s = ext
        span = 1
        while span < w:
            s = s + pltpu.roll(s, span, axis=0)
            span *= 2
        inv_cnt = 1.0 / jnp.minimum(pos + 1, w).astype(F32)
        pooled_sc[:, lanes] = (s[POOL_HALO:, :] * inv_cnt - ext[POOL_HALO:, :]).astype(BF16)

    conv = jnp.concatenate([c_sc[q] for q in range(CONV_GROUPS)], axis=1)
    hb = _layernorm(conv, clg_ref[...], clb_ref[...])
    hb_sc[...] = (hb * _sigmoid(hb)).astype(BF16)

    out = x_ref[...]
    for g in range(POOL_GROUPS):
        lo, hi = g * POOL_OUT_GROUP, (g + 1) * POOL_OUT_GROUP
        ya = _dot(pooled_sc[:, g * POOL_GROUP_DIM:(g + 1) * POOL_GROUP_DIM], pw_ref[g]) * ps_ref[:, lo:hi]
        yb = _dot(hb_sc[...], cpw_ref[:, lo:hi])
        yc = _dot(gated_sc[...], so_ref[:, lo:hi])
        merged = gated(0, lo, hi, ya) + gated(1, lo, hi, yb) + gated(2, lo, hi, yc)
        out = out + _dot((0.5 * merged).astype(BF16), wo_ref[lo:hi, :])
    o_ref[...] = out


def _ffn_kernel(x_ref, nrm_ref, wup_ref, dww_ref, dwb_ref, wdn_ref, nf_ref, o_ref,
                h_sc, g_sc, carry_sc, acc_sc, z_sc, act_sc, *, final_norm):
    t = pl.program_id(1)
    T = FFN_TILE

    @pl.when(t == 0)
    def _():
        carry_sc[...] = jnp.zeros(carry_sc.shape, F32)

    for r0 in range(0, T, HEAD_ROWS):
        h_sc[r0:r0 + HEAD_ROWS, :] = _rmsnorm(x_ref[r0:r0 + HEAD_ROWS, :], nrm_ref[...]).astype(BF16)
    acc_sc[...] = x_ref[...]

    def up(c):
        cols = [wup_ref[:, lo + c * FFN_CHUNK:lo + (c + 1) * FFN_CHUNK] for lo in (0, D_FF)]
        w = jnp.concatenate(cols, axis=1)
        if c == 0:
            for r0 in range(0, T, HEAD_ROWS):
                z_sc[0, r0:r0 + HEAD_ROWS, :] = _dot(h_sc[r0:r0 + HEAD_ROWS, :], w)
        else:
            z_sc[c % 2] = _dot(h_sc[...], w)

    def gate_act(c):
        zr = z_sc.at[c % 2]
        g_sc[0:FFN_HALO, :] = carry_sc[c]
        g_sc[FFN_HALO:FFN_HALO + T, :] = zr[:, :FFN_CHUNK]
        carry_sc[c] = g_sc[T:T + FFN_HALO, :]
        g = dwb_ref[c] + g_sc[FFN_HALO:FFN_HALO + T, :] * dww_ref[c, FFN_CONV_K - 1:FFN_CONV_K, :]
        for k in range(FFN_CONV_K - 1):
            d = FFN_CONV_K - 1 - k
            g = g + g_sc[FFN_HALO - d:FFN_HALO - d + T, :] * dww_ref[c, k:k + 1, :]
        a = 0.5 * g
        act = (a * zr[:, FFN_CHUNK:]) * (jnp.tanh(a) + 1.0)
        act_sc[:, c * FFN_CHUNK:(c + 1) * FFN_CHUNK] = act.astype(BF16)

    up(0)
    for c in range(FFN_NCHUNK):
        if c + 1 < FFN_NCHUNK:
            up(c + 1)
        gate_act(c)
    acc_sc[...] += _dot(act_sc[...], wdn_ref[...])

    y = acc_sc[...]
    if final_norm:
        y = _rmsnorm(y, nf_ref[...])
    o_ref[...] = y


def _const_spec(shape, layer):
    zeros = (0,) * len(shape)
    return pl.BlockSpec((None,) + tuple(shape), lambda b, t: (layer,) + zeros,
                        pipeline_mode=pl.Buffered(1))


def _tile_spec(tile):
    return pl.BlockSpec((None, tile, D_MODEL), lambda b, t: (b, t, 0))


def _compiler_params():
    return pltpu.CompilerParams(dimension_semantics=("arbitrary", "arbitrary"),
                                vmem_limit_bytes=VMEM_LIMIT_BYTES)


def _mixer_call(x, p, layer):
    B, S, _ = x.shape
    T = MIX_TILE
    consts = [p['norm_mix'], p['w_in'], p['b_gate'], p['pool_w'], p['pool_scale'],
              p['conv_dw_w'], p['conv_dw_b'], p['conv_ln_g'], p['conv_ln_b'], p['conv_pw'],
              p['sgu_w'], p['sgu_b'], p['sgu_ln_g'], p['sgu_ln_b'], p['sgu_out'], p['w_out']]
    return pl.pallas_call(
        _mixer_kernel,
        grid=(B, S // T),
        in_specs=[_tile_spec(T)] + [_const_spec(a.shape[1:], layer) for a in consts],
        out_specs=_tile_spec(T),
        out_shape=jax.ShapeDtypeStruct(x.shape, F32),
        scratch_shapes=[
            pltpu.VMEM((T, D_MODEL), BF16),
            pltpu.VMEM((T, D_MODEL), BF16),
            pltpu.VMEM((T + POOL_HALO, POOL_WIDTH), F32),
            pltpu.VMEM((CONV_GROUPS, T + CONV_HALO, LANES), F32),
            pltpu.VMEM((SUBLANES, T + CONV_HALO - SUBLANES, LANES), F32),
            pltpu.VMEM((T, LANES), F32),
            pltpu.VMEM((CONV_HALO, LANES), F32),
            pltpu.VMEM((CONV_GROUPS, T, LANES), F32),
            pltpu.VMEM((T, CONV_WIDTH), BF16),
            pltpu.VMEM((T, SGU_WIDTH), F32),
            pltpu.VMEM((T, SGU_WIDTH), BF16),
            pltpu.VMEM((T, SGU_WIDTH), BF16),
            pltpu.VMEM((CONV_GROUPS, T, GATE_BLOCK), F32),
            pltpu.VMEM((T, POOL_WIDTH), BF16),
        ],
        compiler_params=_compiler_params(),
        name=f"mixer_l{layer}",
    )(x, *consts)


def _ffn_call(x, p, layer, final_norm):
    B, S, _ = x.shape
    T = FFN_TILE
    consts = [p['norm_ffn'], p['ffn_up'], p['ffn_dw_w'], p['ffn_dw_b'], p['ffn_down']]
    nf_spec = pl.BlockSpec((1, D_MODEL), lambda b, t: (0, 0))
    return pl.pallas_call(
        functools.partial(_ffn_kernel, final_norm=final_norm),
        grid=(B, S // T),
        in_specs=[_tile_spec(T)] + [_const_spec(a.shape[1:], layer) for a in consts] + [nf_spec],
        out_specs=_tile_spec(T),
        out_shape=jax.ShapeDtypeStruct(x.shape, F32),
        scratch_shapes=[
            pltpu.VMEM((T, D_MODEL), BF16),
            pltpu.VMEM((T + FFN_HALO, FFN_CHUNK), F32),
            pltpu.VMEM((FFN_NCHUNK, FFN_HALO, FFN_CHUNK), F32),
            pltpu.VMEM((T, D_MODEL), F32),
            pltpu.VMEM((2, T, 2 * FFN_CHUNK), F32),
            pltpu.VMEM((T, D_FF), BF16),
        ],
        compiler_params=_compiler_params(),
        name=f"ffn_l{layer}",
    )(x, *consts, p['norm_final'])


def _prepare_params(norm_mix, w_in, b_gate, pool_w, pool_scale, conv_dw_w, conv_dw_b,
                    conv_ln_g, conv_ln_b, conv_pw, sgu_w, sgu_b, sgu_ln_g, sgu_ln_b, sgu_out,
                    w_out, norm_ffn, ffn_up, ffn_dw_w, ffn_dw_b, ffn_down, norm_final):
    L = DEPTH
    row = lambda a: a.reshape(L, 1, a.shape[-1])
    return {
        'norm_mix': row(norm_mix),
        'w_in': w_in.astype(BF16),
        'b_gate': b_gate.reshape(L, CONV_GROUPS, 1, GATE_BLOCK),
        'pool_w': pool_w.astype(BF16),
        'pool_scale': row(pool_scale),
        'conv_dw_w': conv_dw_w.reshape(L, CONV_K, CONV_GROUPS, LANES).transpose(0, 2, 1, 3),
        'conv_dw_b': conv_dw_b.reshape(L, CONV_GROUPS, 1, LANES),
        'conv_ln_g': row(conv_ln_g),
        'conv_ln_b': row(conv_ln_b),
        'conv_pw': conv_pw.astype(BF16),
        'sgu_w': sgu_w,
        'sgu_b': jnp.repeat(sgu_b.transpose(0, 2, 1), SGU_HEAD_DIM, axis=2),
        'sgu_ln_g': row(sgu_ln_g),
        'sgu_ln_b': row(sgu_ln_b),
        'sgu_out': sgu_out.astype(BF16),
        'w_out': w_out.astype(BF16),
        'norm_ffn': row(norm_ffn),
        'ffn_up': ffn_up.astype(BF16),
        'ffn_dw_w': ffn_dw_w.reshape(L, FFN_CONV_K, FFN_NCHUNK, FFN_CHUNK).transpose(0, 2, 1, 3),
        'ffn_dw_b': ffn_dw_b.reshape(L, FFN_NCHUNK, 1, FFN_CHUNK),
        'ffn_down': ffn_down.astype(BF16),
        'norm_final': norm_final.reshape(1, D_MODEL),
    }


def kernel(x, norm_mix, w_in, b_gate, pool_w, pool_scale, conv_dw_w, conv_dw_b, conv_ln_g, conv_ln_b, conv_pw, sgu_w, sgu_b, sgu_ln_g, sgu_ln_b, sgu_out, w_out, norm_ffn, ffn_up, ffn_dw_w, ffn_dw_b, ffn_down, norm_final):
    assert x.shape[1] % MIX_TILE == 0 and x.shape[1] % FFN_TILE == 0 and x.shape[2] == D_MODEL
    p = _prepare_params(norm_mix, w_in, b_gate, pool_w, pool_scale, conv_dw_w, conv_dw_b,
                        conv_ln_g, conv_ln_b, conv_pw, sgu_w, sgu_b, sgu_ln_g, sgu_ln_b,
                        sgu_out, w_out, norm_ffn, ffn_up, ffn_dw_w, ffn_dw_b, ffn_down,
                        norm_final)
    for layer in range(DEPTH):
        x = _mixer_call(x, p, layer)
        x = _ffn_call(x, p, layer, final_norm=(layer == DEPTH - 1))
    return x
```

```python
import functools
import math

import jax
import jax.numpy as jnp
from jax import lax
from jax.experimental import pallas as pl
from jax.experimental.pallas import tpu as pltpu

D_MODEL = 1024
DEPTH = 4
POOL_GROUPS = 4
POOL_WINDOWS = (2, 4, 8, 16)
POOL_GROUP_DIM = 128
POOL_WIDTH = 512
POOL_OUT_GROUP = 256
CONV_WIDTH = 512
CONV_K = 31
SGU_HEADS = 4
SGU_WIDTH = 512
SGU_HEAD_DIM = 128
SGU_CHUNK = 128
N_BRANCH = 3
D_FF = 2816
FFN_CONV_K = 3
RMS_EPS = 1e-6
LN_EPS = 1e-5

COL_POOL = 0
COL_CV = POOL_WIDTH
COL_CG = COL_CV + CONV_WIDTH
COL_U = COL_CG + CONV_WIDTH
COL_V = COL_U + SGU_WIDTH
COL_GATE = COL_V + SGU_WIDTH

SUBLANES = 8
LANES = 128
MIX_TILE = 512
FFN_TILE = 1024
POOL_HALO = 16
CONV_HALO = 32
CONV_GROUPS = CONV_WIDTH // LANES
HEAD_ROWS = 256
OUT_BLOCK = 512
BUILD_ROWS = 64
CONV_ROWS = 64
GATE_BLOCK = N_BRANCH * D_MODEL // CONV_GROUPS
FFN_HALO = SUBLANES
FFN_CHUNK = 256
FFN_NCHUNK = D_FF // FFN_CHUNK
VMEM_LIMIT_BYTES = 56 * 1024 * 1024

F32 = jnp.float32
BF16 = jnp.bfloat16


def _sigmoid(x):
    return 0.5 * jnp.tanh(0.5 * x) + 0.5


def _gelu_tanh(x):
    c = math.sqrt(2.0 / math.pi)
    return (0.5 * x) * (1.0 + jnp.tanh(x * (c + (c * 0.044715) * (x * x))))


def _rmsnorm(x, g):
    return x * lax.rsqrt(jnp.mean(x * x, axis=-1, keepdims=True) + RMS_EPS) * g


def _layernorm(x, g, b):
    mu = jnp.mean(x, axis=-1, keepdims=True)
    xc = x - mu
    var = jnp.mean(xc * xc, axis=-1, keepdims=True)
    return xc * lax.rsqrt(var + LN_EPS) * g + b


def _dot(a, b):
    return jnp.dot(a, b, preferred_element_type=F32)


def _gate_pieces(lo, hi):
    out = []
    while lo < hi:
        blk, off = divmod(lo, GATE_BLOCK)
        width = min(hi - lo, GATE_BLOCK - off)
        out.append((blk, off, width))
        lo += width
    return out


def _mixer_kernel(x_ref, nrm_ref, win_ref, bg_ref, pw_ref, ps_ref, cdw_ref, cdb_ref,
                  clg_ref, clb_ref, cpw_ref, sw_ref, sb_ref, slg_ref, slb_ref, so_ref,
                  wo_ref, o_ref, h_sc, pool_sc, conv_sc, shift_sc, ctmp_sc, wtmp_sc, c_sc,
                  hb_sc, u_sc, v_sc, gated_sc, gate_sc, pooled_sc):
    t = pl.program_id(1)
    T = MIX_TILE

    @pl.when(t == 0)
    def _():
        pool_sc[0:POOL_HALO, :] = jnp.zeros((POOL_HALO, POOL_WIDTH), F32)
        conv_sc[:, 0:CONV_HALO, :] = jnp.zeros((CONV_GROUPS, CONV_HALO, LANES), F32)

    @pl.when(t > 0)
    def _():
        pool_sc[0:POOL_HALO, :] = pool_sc[T:T + POOL_HALO, :]
        conv_sc[:, 0:CONV_HALO, :] = conv_sc[:, T:T + CONV_HALO, :]

    h_sc[...] = _rmsnorm(x_ref[...], nrm_ref[...]).astype(BF16)

    def proj(lo, width):
        return _dot(h_sc[...], win_ref[:, lo:lo + width])

    for lo in range(0, CONV_WIDTH, 2 * LANES):
        w = jnp.concatenate([win_ref[:, COL_CV + lo:COL_CV + lo + 2 * LANES],
                             win_ref[:, COL_CG + lo:COL_CG + lo + 2 * LANES]], axis=1)
        z = _dot(h_sc[...], w)
        glu = z[:, :2 * LANES] * _sigmoid(z[:, 2 * LANES:])
        for k in range(2):
            q = lo // LANES + k
            conv_sc[q, CONV_HALO:CONV_HALO + T, :] = glu[:, k * LANES:(k + 1) * LANES]
    v = _gelu_tanh(proj(COL_V, SGU_WIDTH))
    v_sc[...] = _layernorm(v, slg_ref[...], slb_ref[...]).astype(BF16)
    u_sc[...] = _gelu_tanh(proj(COL_U, SGU_WIDTH))
    pool_sc[POOL_HALO:POOL_HALO + T, :] = proj(COL_POOL, POOL_WIDTH)

    row = lax.broadcasted_iota(jnp.int32, (SGU_CHUNK, SGU_CHUNK), 0)
    col = lax.broadcasted_iota(jnp.int32, (SGU_CHUNK, SGU_CHUNK), 1)
    for hd in range(SGU_HEADS):
        lanes = slice(hd * SGU_HEAD_DIM, (hd + 1) * SGU_HEAD_DIM)
        ws = jnp.where(row >= col, sw_ref[hd], 0.0).astype(BF16)
        for n in range(T // SGU_CHUNK):
            rows = slice(n * SGU_CHUNK, (n + 1) * SGU_CHUNK)
            mixed = _dot(ws, v_sc[rows, lanes]) + sb_ref[:, lanes]
            gated_sc[rows, lanes] = (u_sc[rows, lanes] * mixed).astype(BF16)

    def gate_conv_step(q, _):
        for c0 in range(0, T + CONV_HALO - SUBLANES, BUILD_ROWS):
            n = min(BUILD_ROWS, T + CONV_HALO - SUBLANES - c0)
            ext = conv_sc[q, c0:c0 + n + SUBLANES, :]
            shift_sc[0, c0:c0 + n, :] = ext[SUBLANES:, :]
            for r in range(1, SUBLANES):
                shift_sc[r, c0:c0 + n, :] = pltpu.roll(ext, r, axis=0)[SUBLANES:, :]
        wtmp_sc[0:CONV_K, :] = cdw_ref[q]
        wtmp_sc[CONV_K:CONV_K + 1, :] = cdb_ref[q]
        for r0 in range(0, T, CONV_ROWS):
            acc = wtmp_sc[CONV_K:CONV_K + 1, :]
            for d in range(CONV_K):
                a, r = divmod(d, SUBLANES)
                lo = r0 + CONV_HALO - SUBLANES * (a + 1)
                acc = acc + shift_sc[r, lo:lo + CONV_ROWS, :] * wtmp_sc[CONV_K - 1 - d:CONV_K - d, :]
            ctmp_sc[r0:r0 + CONV_ROWS, :] = acc
        col = pl.multiple_of(COL_GATE + q * GATE_BLOCK, 2 * LANES)
        gate_sc[q] = jnp.tanh(0.5 * (_dot(h_sc[...], win_ref[:, pl.ds(col, GATE_BLOCK)]) + bg_ref[q])) + 1.0
        c_sc[q] = ctmp_sc[...]
        return 0

    lax.fori_loop(0, CONV_GROUPS, gate_conv_step, 0)

    def gated(i, lo, hi, y):
        parts = [gate_sc[blk, :, off:off + w] for blk, off, w in
                 _gate_pieces(i * D_MODEL + lo, i * D_MODEL + hi)]
        g = parts[0] if len(parts) == 1 else jnp.concatenate(parts, axis=1)
        return g * y

    pos = t * T + lax.broadcasted_iota(jnp.int32, (T, 1), 0)
    for g, w in enumerate(POOL_WINDOWS):
        lanes = slice(g * POOL_GROUP_DIM, (g + 1) * POOL_GROUP_DIM)
        ext = pool_sc[:, lanes]
        s = ext
        span = 1
        while span < w:
            s = s + pltpu.roll(s, span, axis=0)
            span *= 2
        inv_cnt = 1.0 / jnp.minimum(pos + 1, w).astype(F32)
        pooled_sc[:, lanes] = (s[POOL_HALO:, :] * inv_cnt - ext[POOL_HALO:, :]).astype(BF16)

    conv = jnp.concatenate([c_sc[q] for q in range(CONV_GROUPS)], axis=1)
    hb = _layernorm(conv, clg_ref[...], clb_ref[...])
    hb_sc[...] = (hb * _sigmoid(hb)).astype(BF16)

    out = x_ref[...]
    for lo in range(0, D_MODEL, OUT_BLOCK):
        hi = lo + OUT_BLOCK
        ya = jnp.concatenate(
            [_dot(pooled_sc[:, g * POOL_GROUP_DIM:(g + 1) * POOL_GROUP_DIM], pw_ref[g])
             for g in range(lo // POOL_OUT_GROUP, hi // POOL_OUT_GROUP)], axis=1) * ps_ref[:, lo:hi]
        yb = _dot(hb_sc[...], cpw_ref[:, lo:hi])
        yc = _dot(gated_sc[...], so_ref[:, lo:hi])
        merged = gated(0, lo, hi, ya) + gated(1, lo, hi, yb) + gated(2, lo, hi, yc)
        out = out + _dot((0.5 * merged).astype(BF16), wo_ref[lo:hi, :])
    o_ref[...] = out


def _ffn_kernel(x_ref, nrm_ref, wup_ref, dww_ref, dwb_ref, wdn_ref, nf_ref, o_ref,
                h_sc, g_sc, carry_sc, acc_sc, z_sc, act_sc, *, final_norm):
    t = pl.program_id(1)
    T = FFN_TILE

    @pl.when(t == 0)
    def _():
        carry_sc[...] = jnp.zeros(carry_sc.shape, F32)

    for r0 in range(0, T, HEAD_ROWS):
        h_sc[r0:r0 + HEAD_ROWS, :] = _rmsnorm(x_ref[r0:r0 + HEAD_ROWS, :], nrm_ref[...]).astype(BF16)
    acc_sc[...] = x_ref[...]

    def up(c):
        cols = [wup_ref[:, lo + c * FFN_CHUNK:lo + (c + 1) * FFN_CHUNK] for lo in (0, D_FF)]
        w = jnp.concatenate(cols, axis=1)
        if c == 0:
            for r0 in range(0, T, HEAD_ROWS):
                z_sc[0, r0:r0 + HEAD_ROWS, :] = _dot(h_sc[r0:r0 + HEAD_ROWS, :], w)
        else:
            z_sc[c % 2] = _dot(h_sc[...], w)

    def gate_act(c):
        zr = z_sc.at[c % 2]
        g_sc[0:FFN_HALO, :] = carry_sc[c]
        g_sc[FFN_HALO:FFN_HALO + T, :] = zr[:, :FFN_CHUNK]
        carry_sc[c] = g_sc[T:T + FFN_HALO, :]
        g = dwb_ref[c] + g_sc[FFN_HALO:FFN_HALO + T, :] * dww_ref[c, FFN_CONV_K - 1:FFN_CONV_K, :]
        for k in range(FFN_CONV_K - 1):
            d = FFN_CONV_K - 1 - k
            g = g + g_sc[FFN_HALO - d:FFN_HALO - d + T, :] * dww_ref[c, k:k + 1, :]
        a = 0.5 * g
        act = (a * zr[:, FFN_CHUNK:]) * (jnp.tanh(a) + 1.0)
        act_sc[:, c * FFN_CHUNK:(c + 1) * FFN_CHUNK] = act.astype(BF16)

    up(0)
    for c in range(FFN_NCHUNK):
        if c + 1 < FFN_NCHUNK:
            up(c + 1)
        gate_act(c)
    acc_sc[...] += _dot(act_sc[...], wdn_ref[...])

    y = acc_sc[...]
    if final_norm:
        y = _rmsnorm(y, nf_ref[...])
    o_ref[...] = y


def _const_spec(shape, layer):
    zeros = (0,) * len(shape)
    return pl.BlockSpec((None,) + tuple(shape), lambda b, t: (layer,) + zeros,
                        pipeline_mode=pl.Buffered(1))


def _tile_spec(tile):
    return pl.BlockSpec((None, tile, D_MODEL), lambda b, t: (b, t, 0))


def _compiler_params():
    return pltpu.CompilerParams(dimension_semantics=("arbitrary", "arbitrary"),
                                vmem_limit_bytes=VMEM_LIMIT_BYTES)


def _mixer_call(x, p, layer):
    B, S, _ = x.shape
    T = MIX_TILE
    consts = [p['norm_mix'], p['w_in'], p['b_gate'], p['pool_w'], p['pool_scale'],
              p['conv_dw_w'], p['conv_dw_b'], p['conv_ln_g'], p['conv_ln_b'], p['conv_pw'],
              p['sgu_w'], p['sgu_b'], p['sgu_ln_g'], p['sgu_ln_b'], p['sgu_out'], p['w_out']]
    return pl.pallas_call(
        _mixer_kernel,
        grid=(B, S // T),
        in_specs=[_tile_spec(T)] + [_const_spec(a.shape[1:], layer) for a in consts],
        out_specs=_tile_spec(T),
        out_shape=jax.ShapeDtypeStruct(x.shape, F32),
        scratch_shapes=[
            pltpu.VMEM((T, D_MODEL), BF16),
            pltpu.VMEM((T + POOL_HALO, POOL_WIDTH), F32),
            pltpu.VMEM((CONV_GROUPS, T + CONV_HALO, LANES), F32),
            pltpu.VMEM((SUBLANES, T + CONV_HALO - SUBLANES, LANES), F32),
            pltpu.VMEM((T, LANES), F32),
            pltpu.VMEM((CONV_HALO, LANES), F32),
            pltpu.VMEM((CONV_GROUPS, T, LANES), F32),
            pltpu.VMEM((T, CONV_WIDTH), BF16),
            pltpu.VMEM((T, SGU_WIDTH), F32),
            pltpu.VMEM((T, SGU_WIDTH), BF16),
            pltpu.VMEM((T, SGU_WIDTH), BF16),
            pltpu.VMEM((CONV_GROUPS, T, GATE_BLOCK), F32),
            pltpu.VMEM((T, POOL_WIDTH), BF16),
        ],
        compiler_params=_compiler_params(),
        name=f"mixer_l{layer}",
    )(x, *consts)


def _ffn_call(x, p, layer, final_norm):
    B, S, _ = x.shape
    T = FFN_TILE
    consts = [p['norm_ffn'], p['ffn_up'], p['ffn_dw_w'], p['ffn_dw_b'], p['ffn_down']]
    nf_spec = pl.BlockSpec((1, D_MODEL), lambda b, t: (0, 0))
    return pl.pallas_call(
        functools.partial(_ffn_kernel, final_norm=final_norm),
        grid=(B, S // T),
        in_specs=[_tile_spec(T)] + [_const_spec(a.shape[1:], layer) for a in consts] + [nf_spec],
        out_specs=_tile_spec(T),
        out_shape=jax.ShapeDtypeStruct(x.shape, F32),
        scratch_shapes=[
            pltpu.VMEM((T, D_MODEL), BF16),
            pltpu.VMEM((T + FFN_HALO, FFN_CHUNK), F32),
            pltpu.VMEM((FFN_NCHUNK, FFN_HALO, FFN_CHUNK), F32),
            pltpu.VMEM((T, D_MODEL), F32),
            pltpu.VMEM((2, T, 2 * FFN_CHUNK), F32),
            pltpu.VMEM((T, D_FF), BF16),
        ],
        compiler_params=_compiler_params(),
        name=f"ffn_l{layer}",
    )(x, *consts, p['norm_final'])


def _prepare_params(norm_mix, w_in, b_gate, pool_w, pool_scale, conv_dw_w, conv_dw_b,
                    conv_ln_g, conv_ln_b, conv_pw, sgu_w, sgu_b, sgu_ln_g, sgu_ln_b, sgu_out,
                    w_out, norm_ffn, ffn_up, ffn_dw_w, ffn_dw_b, ffn_down, norm_final):
    L = DEPTH
    row = lambda a: a.reshape(L, 1, a.shape[-1])
    return {
        'norm_mix': row(norm_mix),
        'w_in': w_in.astype(BF16),
        'b_gate': b_gate.reshape(L, CONV_GROUPS, 1, GATE_BLOCK),
        'pool_w': pool_w.astype(BF16),
        'pool_scale': row(pool_scale),
        'conv_dw_w': conv_dw_w.reshape(L, CONV_K, CONV_GROUPS, LANES).transpose(0, 2, 1, 3),
        'conv_dw_b': conv_dw_b.reshape(L, CONV_GROUPS, 1, LANES),
        'conv_ln_g': row(conv_ln_g),
        'conv_ln_b': row(conv_ln_b),
        'conv_pw': conv_pw.astype(BF16),
        'sgu_w': sgu_w,
        'sgu_b': jnp.repeat(sgu_b.transpose(0, 2, 1), SGU_HEAD_DIM, axis=2),
        'sgu_ln_g': row(sgu_ln_g),
        'sgu_ln_b': row(sgu_ln_b),
        'sgu_out': sgu_out.astype(BF16),
        'w_out': w_out.astype(BF16),
        'norm_ffn': row(norm_ffn),
        'ffn_up': ffn_up.astype(BF16),
        'ffn_dw_w': ffn_dw_w.reshape(L, FFN_CONV_K, FFN_NCHUNK, FFN_CHUNK).transpose(0, 2, 1, 3),
        'ffn_dw_b': ffn_dw_b.reshape(L, FFN_NCHUNK, 1, FFN_CHUNK),
        'ffn_down': ffn_down.astype(BF16),
        'norm_final': norm_final.reshape(1, D_MODEL),
    }


def kernel(x, norm_mix, w_in, b_gate, pool_w, pool_scale, conv_dw_w, conv_dw_b, conv_ln_g, conv_ln_b, conv_pw, sgu_w, sgu_b, sgu_ln_g, sgu_ln_b, sgu_out, w_out, norm_ffn, ffn_up, ffn_dw_w, ffn_dw_b, ffn_down, norm_final):
    assert x.shape[1] % MIX_TILE == 0 and x.shape[1] % FFN_TILE == 0 and x.shape[2] == D_MODEL
    p = _prepare_params(norm_mix, w_in, b_gate, pool_w, pool_scale, conv_dw_w, conv_dw_b,
                        conv_ln_g, conv_ln_b, conv_pw, sgu_w, sgu_b, sgu_ln_g, sgu_ln_b,
                        sgu_out, w_out, norm_ffn, ffn_up, ffn_dw_w, ffn_dw_b, ffn_down,
                        norm_final)
    for layer in range(DEPTH):
        x = _mixer_call(x, p, layer)
        x = _ffn_call(x, p, layer, final_norm=(layer == DEPTH - 1))
    return x
```

```python
import functools
import math

import jax
import jax.numpy as jnp
from jax import lax
from jax.experimental import pallas as pl
from jax.experimental.pallas import tpu as pltpu

D_MODEL = 1024
DEPTH = 4
POOL_GROUPS = 4
POOL_WINDOWS = (2, 4, 8, 16)
POOL_GROUP_DIM = 128
POOL_WIDTH = 512
POOL_OUT_GROUP = 256
CONV_WIDTH = 512
CONV_K = 31
SGU_HEADS = 4
SGU_WIDTH = 512
SGU_HEAD_DIM = 128
SGU_CHUNK = 128
N_BRANCH = 3
D_FF = 2816
FFN_CONV_K = 3
RMS_EPS = 1e-6
LN_EPS = 1e-5

COL_POOL = 0
COL_CV = POOL_WIDTH
COL_CG = COL_CV + CONV_WIDTH
COL_U = COL_CG + CONV_WIDTH
COL_V = COL_U + SGU_WIDTH
COL_GATE = COL_V + SGU_WIDTH

SUBLANES = 8
LANES = 128
MIX_TILE = 512
FFN_TILE = 1024
POOL_HALO = 16
CONV_HALO = 32
CONV_GROUPS = CONV_WIDTH // LANES
HEAD_ROWS = 256
BUILD_ROWS = 64
CONV_ROWS = 64
GATE_BLOCK = N_BRANCH * D_MODEL // CONV_GROUPS
FFN_HALO = SUBLANES
FFN_CHUNK = 256
FFN_NCHUNK = D_FF // FFN_CHUNK
VMEM_LIMIT_BYTES = 56 * 1024 * 1024

F32 = jnp.float32
BF16 = jnp.bfloat16


def _sigmoid(x):
    return 0.5 * jnp.tanh(0.5 * x) + 0.5


def _gelu_tanh(x):
    c = math.sqrt(2.0 / math.pi)
    return (0.5 * x) * (1.0 + jnp.tanh(x * (c + (c * 0.044715) * (x * x))))


def _rmsnorm(x, g):
    return x * lax.rsqrt(jnp.mean(x * x, axis=-1, keepdims=True) + RMS_EPS) * g


def _layernorm(x, g, b):
    mu = jnp.mean(x, axis=-1, keepdims=True)
    xc = x - mu
    var = jnp.mean(xc * xc, axis=-1, keepdims=True)
    return xc * lax.rsqrt(var + LN_EPS) * g + b


def _dot(a, b):
    return jnp.dot(a, b, preferred_element_type=F32)


def _gate_pieces(lo, hi):
    out = []
    while lo < hi:
        blk, off = divmod(lo, GATE_BLOCK)
        width = min(hi - lo, GATE_BLOCK - off)
        out.append((blk, off, width))
        lo += width
    return out


def _mixer_kernel(x_ref, nrm_ref, win_ref, bg_ref, pw_ref, ps_ref, cdw_ref, cdb_ref,
                  clg_ref, clb_ref, cpw_ref, sw_ref, sb_ref, slg_ref, slb_ref, so_ref,
                  wo_ref, o_ref, h_sc, pool_sc, conv_sc, shift_sc, ctmp_sc, wtmp_sc, c_sc,
                  hb_sc, u_sc, v_sc, gated_sc, gate_sc, pooled_sc):
    t = pl.program_id(1)
    T = MIX_TILE

    @pl.when(t == 0)
    def _():
        pool_sc[0:POOL_HALO, :] = jnp.zeros((POOL_HALO, POOL_WIDTH), F32)
        conv_sc[:, 0:CONV_HALO, :] = jnp.zeros((CONV_GROUPS, CONV_HALO, LANES), F32)

    @pl.when(t > 0)
    def _():
        pool_sc[0:POOL_HALO, :] = pool_sc[T:T + POOL_HALO, :]
        conv_sc[:, 0:CONV_HALO, :] = conv_sc[:, T:T + CONV_HALO, :]

    h_sc[...] = _rmsnorm(x_ref[...], nrm_ref[...]).astype(BF16)

    def proj(lo, width):
        return _dot(h_sc[...], win_ref[:, lo:lo + width])

    for lo in range(0, CONV_WIDTH, 2 * LANES):
        w = jnp.concatenate([win_ref[:, COL_CV + lo:COL_CV + lo + 2 * LANES],
                             win_ref[:, COL_CG + lo:COL_CG + lo + 2 * LANES]], axis=1)
        z = _dot(h_sc[...], w)
        glu = z[:, :2 * LANES] * _sigmoid(z[:, 2 * LANES:])
        for k in range(2):
            q = lo // LANES + k
            conv_sc[q, CONV_HALO:CONV_HALO + T, :] = glu[:, k * LANES:(k + 1) * LANES]
    v = _gelu_tanh(proj(COL_V, SGU_WIDTH))
    v_sc[...] = _layernorm(v, slg_ref[...], slb_ref[...]).astype(BF16)
    u_sc[...] = _gelu_tanh(proj(COL_U, SGU_WIDTH))
    pool_sc[POOL_HALO:POOL_HALO + T, :] = proj(COL_POOL, POOL_WIDTH)

    row = lax.broadcasted_iota(jnp.int32, (SGU_CHUNK, SGU_CHUNK), 0)
    col = lax.broadcasted_iota(jnp.int32, (SGU_CHUNK, SGU_CHUNK), 1)
    for hd in range(SGU_HEADS):
        lanes = slice(hd * SGU_HEAD_DIM, (hd + 1) * SGU_HEAD_DIM)
        ws = jnp.where(row >= col, sw_ref[hd], 0.0).astype(BF16)
        for n in range(T // SGU_CHUNK):
            rows = slice(n * SGU_CHUNK, (n + 1) * SGU_CHUNK)
            mixed = _dot(ws, v_sc[rows, lanes]) + sb_ref[:, lanes]
            gated_sc[rows, lanes] = (u_sc[rows, lanes] * mixed).astype(BF16)

    def gate_conv_step(q, _):
        for c0 in range(0, T + CONV_HALO - SUBLANES, BUILD_ROWS):
            n = min(BUILD_ROWS, T + CONV_HALO - SUBLANES - c0)
            ext = conv_sc[q, c0:c0 + n + SUBLANES, :]
            shift_sc[0, c0:c0 + n, :] = ext[SUBLANES:, :]
            for r in range(1, SUBLANES):
                shift_sc[r, c0:c0 + n, :] = pltpu.roll(ext, r, axis=0)[SUBLANES:, :]
        wtmp_sc[0:CONV_K, :] = cdw_ref[q]
        wtmp_sc[CONV_K:CONV_K + 1, :] = cdb_ref[q]
        for r0 in range(0, T, CONV_ROWS):
            acc = wtmp_sc[CONV_K:CONV_K + 1, :]
            for d in range(CONV_K):
                a, r = divmod(d, SUBLANES)
                lo = r0 + CONV_HALO - SUBLANES * (a + 1)
                acc = acc + shift_sc[r, lo:lo + CONV_ROWS, :] * wtmp_sc[CONV_K - 1 - d:CONV_K - d, :]
            ctmp_sc[r0:r0 + CONV_ROWS, :] = acc
        col = pl.multiple_of(COL_GATE + q * GATE_BLOCK, 2 * LANES)
        gate_sc[q] = jnp.tanh(0.5 * (_dot(h_sc[...], win_ref[:, pl.ds(col, GATE_BLOCK)]) + bg_ref[q])) + 1.0
        c_sc[q] = ctmp_sc[...]
        return 0

    lax.fori_loop(0, CONV_GROUPS, gate_conv_step, 0)

    def gated(i, lo, hi, y):
        parts = [gate_sc[blk, :, off:off + w] for blk, off, w in
                 _gate_pieces(i * D_MODEL + lo, i * D_MODEL + hi)]
        g = parts[0] if len(parts) == 1 else jnp.concatenate(parts, axis=1)
        return g * y

    pos = t * T + lax.broadcasted_iota(jnp.int32, (T, 1), 0)
    for g, w in enumerate(POOL_WINDOWS):
        lanes = slice(g * POOL_GROUP_DIM, (g + 1) * POOL_GROUP_DIM)
        ext = pool_sc[:, lanes]
        s = ext
        span = 1
        while span < w:
            s = s + pltpu.roll(s, span, axis=0)
            span *= 2
        inv_cnt = 1.0 / jnp.minimum(pos + 1, w).astype(F32)
        pooled_sc[:, lanes] = (s[POOL_HALO:, :] * inv_cnt - ext[POOL_HALO:, :]).astype(BF16)

    conv = jnp.concatenate([c_sc[q] for q in range(CONV_GROUPS)], axis=1)
    hb = _layernorm(conv, clg_ref[...], clb_ref[...])
    hb_sc[...] = (hb * _sigmoid(hb)).astype(BF16)

    out = x_ref[...]
    for g in range(POOL_GROUPS):
        lo, hi = g * POOL_OUT_GROUP, (g + 1) * POOL_OUT_GROUP
        ya = _dot(pooled_sc[:, g * POOL_GROUP_DIM:(g + 1) * POOL_GROUP_DIM], pw_ref[g]) * ps_ref[:, lo:hi]
        yb = _dot(hb_sc[...], cpw_ref[:, lo:hi])
        yc = _dot(gated_sc[...], so_ref[:, lo:hi])
        merged = gated(0, lo, hi, ya) + gated(1, lo, hi, yb) + gated(2, lo, hi, yc)
        out = out + _dot((0.5 * merged).astype(BF16), wo_ref[lo:hi, :])
    o_ref[...] = out


def _ffn_kernel(x_ref, nrm_ref, wup_ref, dww_ref, dwb_ref, wdn_ref, nf_ref, o_ref,
                h_sc, g_sc, carry_sc, acc_sc, z_sc, act_sc, *, final_norm):
    t = pl.program_id(1)
    T = FFN_TILE

    @pl.when(t == 0)
    def _():
        carry_sc[...] = jnp.zeros(carry_sc.shape, F32)

    for r0 in range(0, T, HEAD_ROWS):
        h_sc[r0:r0 + HEAD_ROWS, :] = _rmsnorm(x_ref[r0:r0 + HEAD_ROWS, :], nrm_ref[...]).astype(BF16)
    acc_sc[...] = x_ref[...]

    def up(c):
        cols = [wup_ref[:, lo + c * FFN_CHUNK:lo + (c + 1) * FFN_CHUNK] for lo in (0, D_FF)]
        w = jnp.concatenate(cols, axis=1)
        blocks = range(0, T, HEAD_ROWS) if c == 0 else (0,)
        rows = HEAD_ROWS if c == 0 else T
        for r0 in blocks:
            z = _dot(h_sc[r0:r0 + rows, :], w)
            g_sc[c % 2, FFN_HALO + r0:FFN_HALO + r0 + rows, :] = z[:, :FFN_CHUNK]
            z_sc[c % 2, r0:r0 + rows, :] = z[:, FFN_CHUNK:]

    def gate_act(c):
        gs = g_sc.at[c % 2]
        gs[0:FFN_HALO, :] = carry_sc[c]
        carry_sc[c] = gs[T:T + FFN_HALO, :]
        g = dwb_ref[c] + gs[FFN_HALO:FFN_HALO + T, :] * dww_ref[c, FFN_CONV_K - 1:FFN_CONV_K, :]
        for k in range(FFN_CONV_K - 1):
            d = FFN_CONV_K - 1 - k
            g = g + gs[FFN_HALO - d:FFN_HALO - d + T, :] * dww_ref[c, k:k + 1, :]
        a = 0.5 * g
        act = (a * z_sc[c % 2]) * (jnp.tanh(a) + 1.0)
        act_sc[:, c * FFN_CHUNK:(c + 1) * FFN_CHUNK] = act.astype(BF16)

    up(0)
    for c in range(FFN_NCHUNK):
        if c + 1 < FFN_NCHUNK:
            up(c + 1)
        gate_act(c)
    acc_sc[...] += _dot(act_sc[...], wdn_ref[...])

    y = acc_sc[...]
    if final_norm:
        y = _rmsnorm(y, nf_ref[...])
    o_ref[...] = y


def _const_spec(shape, layer):
    zeros = (0,) * len(shape)
    return pl.BlockSpec((None,) + tuple(shape), lambda b, t: (layer,) + zeros,
                        pipeline_mode=pl.Buffered(1))


def _tile_spec(tile):
    return pl.BlockSpec((None, tile, D_MODEL), lambda b, t: (b, t, 0))


def _compiler_params():
    return pltpu.CompilerParams(dimension_semantics=("arbitrary", "arbitrary"),
                                vmem_limit_bytes=VMEM_LIMIT_BYTES)


def _mixer_call(x, p, layer):
    B, S, _ = x.shape
    T = MIX_TILE
    consts = [p['norm_mix'], p['w_in'], p['b_gate'], p['pool_w'], p['pool_scale'],
              p['conv_dw_w'], p['conv_dw_b'], p['conv_ln_g'], p['conv_ln_b'], p['conv_pw'],
              p['sgu_w'], p['sgu_b'], p['sgu_ln_g'], p['sgu_ln_b'], p['sgu_out'], p['w_out']]
    return pl.pallas_call(
        _mixer_kernel,
        grid=(B, S // T),
        in_specs=[_tile_spec(T)] + [_const_spec(a.shape[1:], layer) for a in consts],
        out_specs=_tile_spec(T),
        out_shape=jax.ShapeDtypeStruct(x.shape, F32),
        scratch_shapes=[
            pltpu.VMEM((T, D_MODEL), BF16),
            pltpu.VMEM((T + POOL_HALO, POOL_WIDTH), F32),
            pltpu.VMEM((CONV_GROUPS, T + CONV_HALO, LANES), F32),
            pltpu.VMEM((SUBLANES, T + CONV_HALO - SUBLANES, LANES), F32),
            pltpu.VMEM((T, LANES), F32),
            pltpu.VMEM((CONV_HALO, LANES), F32),
            pltpu.VMEM((CONV_GROUPS, T, LANES), F32),
            pltpu.VMEM((T, CONV_WIDTH), BF16),
            pltpu.VMEM((T, SGU_WIDTH), F32),
            pltpu.VMEM((T, SGU_WIDTH), BF16),
            pltpu.VMEM((T, SGU_WIDTH), BF16),
            pltpu.VMEM((CONV_GROUPS, T, GATE_BLOCK), F32),
            pltpu.VMEM((T, POOL_WIDTH), BF16),
        ],
        compiler_params=_compiler_params(),
        name=f"mixer_l{layer}",
    )(x, *consts)


def _ffn_call(x, p, layer, final_norm):
    B, S, _ = x.shape
    T = FFN_TILE
    consts = [p['norm_ffn'], p['ffn_up'], p['ffn_dw_w'], p['ffn_dw_b'], p['ffn_down']]
    nf_spec = pl.BlockSpec((1, D_MODEL), lambda b, t: (0, 0))
    return pl.pallas_call(
        functools.partial(_ffn_kernel, final_norm=final_norm),
        grid=(B, S // T),
        in_specs=[_tile_spec(T)] + [_const_spec(a.shape[1:], layer) for a in consts] + [nf_spec],
        out_specs=_tile_spec(T),
        out_shape=jax.ShapeDtypeStruct(x.shape, F32),
        scratch_shapes=[
            pltpu.VMEM((T, D_MODEL), BF16),
            pltpu.VMEM((2, T + FFN_HALO, FFN_CHUNK), F32),
            pltpu.VMEM((FFN_NCHUNK, FFN_HALO, FFN_CHUNK), F32),
            pltpu.VMEM((T, D_MODEL), F32),
            pltpu.VMEM((2, T, FFN_CHUNK), F32),
            pltpu.VMEM((T, D_FF), BF16),
        ],
        compiler_params=_compiler_params(),
        name=f"ffn_l{layer}",
    )(x, *consts, p['norm_final'])


def _prepare_params(norm_mix, w_in, b_gate, pool_w, pool_scale, conv_dw_w, conv_dw_b,
                    conv_ln_g, conv_ln_b, conv_pw, sgu_w, sgu_b, sgu_ln_g, sgu_ln_b, sgu_out,
                    w_out, norm_ffn, ffn_up, ffn_dw_w, ffn_dw_b, ffn_down, norm_final):
    L = DEPTH
    row = lambda a: a.reshape(L, 1, a.shape[-1])
    return {
        'norm_mix': row(norm_mix),
        'w_in': w_in.astype(BF16),
        'b_gate': b_gate.reshape(L, CONV_GROUPS, 1, GATE_BLOCK),
        'pool_w': pool_w.astype(BF16),
        'pool_scale': row(pool_scale),
        'conv_dw_w': conv_dw_w.reshape(L, CONV_K, CONV_GROUPS, LANES).transpose(0, 2, 1, 3),
        'conv_dw_b': conv_dw_b.reshape(L, CONV_GROUPS, 1, LANES),
        'conv_ln_g': row(conv_ln_g),
        'conv_ln_b': row(conv_ln_b),
        'conv_pw': conv_pw.astype(BF16),
        'sgu_w': sgu_w,
        'sgu_b': jnp.repeat(sgu_b.transpose(0, 2, 1), SGU_HEAD_DIM, axis=2),
        'sgu_ln_g': row(sgu_ln_g),
        'sgu_ln_b': row(sgu_ln_b),
        'sgu_out': sgu_out.astype(BF16),
        'w_out': w_out.astype(BF16),
        'norm_ffn': row(norm_ffn),
        'ffn_up': ffn_up.astype(BF16),
        'ffn_dw_w': ffn_dw_w.reshape(L, FFN_CONV_K, FFN_NCHUNK, FFN_CHUNK).transpose(0, 2, 1, 3),
        'ffn_dw_b': ffn_dw_b.reshape(L, FFN_NCHUNK, 1, FFN_CHUNK),
        'ffn_down': ffn_down.astype(BF16),
        'norm_final': norm_final.reshape(1, D_MODEL),
    }


def kernel(x, norm_mix, w_in, b_gate, pool_w, pool_scale, conv_dw_w, conv_dw_b, conv_ln_g, conv_ln_b, conv_pw, sgu_w, sgu_b, sgu_ln_g, sgu_ln_b, sgu_out, w_out, norm_ffn, ffn_up, ffn_dw_w, ffn_dw_b, ffn_down, norm_final):
    assert x.shape[1] % MIX_TILE == 0 and x.shape[1] % FFN_TILE == 0 and x.shape[2] == D_MODEL
    p = _prepare_params(norm_mix, w_in, b_gate, pool_w, pool_scale, conv_dw_w, conv_dw_b,
                        conv_ln_g, conv_ln_b, conv_pw, sgu_w, sgu_b, sgu_ln_g, sgu_ln_b,
                        sgu_out, w_out, norm_ffn, ffn_up, ffn_dw_w, ffn_dw_b, ffn_down,
                        norm_final)
    for layer in range(DEPTH):
        x = _mixer_call(x, p, layer)
        x = _ffn_call(x, p, layer, final_norm=(layer == DEPTH - 1))
    return x
```
